```python
import math
import jax, jax.numpy as jnp
from jax import lax
import numpy as np

D_MODEL = 1024
BATCH = 4
SEQ = 4096
DEPTH = 2

CHUNK = 64
QBLOCK = 128
HEAD_DIM = 64
HEADS_A = 8
HEADS_B = 8
WIDTH_A = HEADS_A * HEAD_DIM
WIDTH_B = HEADS_B * HEAD_DIM
D_MIX = WIDTH_A + WIDTH_B
IDX_HEADS = 8
IDX_DIM = 64
TOPK_MAX = 256
N_BUCKETS = 32
MAX_DISTANCE = 128
ALPHA = (2.0 * DEPTH) ** 0.25
DN_BETA = (8.0 * DEPTH) ** -0.25
LN_EPS = 1e-5

SPLIT_SIZES = (WIDTH_A, WIDTH_A, WIDTH_A, WIDTH_A,
               WIDTH_B, WIDTH_B, WIDTH_B, WIDTH_B,
               IDX_HEADS * IDX_DIM, IDX_DIM, IDX_HEADS)
N_IN = int(sum(SPLIT_SIZES))
SPLIT_POINTS = tuple(int(v) for v in np.cumsum(SPLIT_SIZES)[:-1])
V_COLS = ((2 * WIDTH_A, 3 * WIDTH_A), (4 * WIDTH_A + 2 * WIDTH_B, 4 * WIDTH_A + 3 * WIDTH_B))

kernel_name = "hybrid_stickbreak_dsa_deepnorm"


def layer_norm(h, g, b):
    hf = h.astype(jnp.float32)
    mu = jnp.mean(hf, axis=-1, keepdims=True)
    var = jnp.mean(jnp.square(hf - mu), axis=-1, keepdims=True)
    out = (hf - mu) * lax.rsqrt(var + LN_EPS) * g.astype(jnp.float32) + b.astype(jnp.float32)
    return out.astype(h.dtype)


def t5_bucket(rel):
    half = N_BUCKETS // 2
    max_exact = half // 2
    ret = jnp.where(rel > 0, half, 0)
    n = jnp.abs(rel)
    nf = jnp.maximum(n, 1).astype(jnp.float32)
    large = max_exact + (jnp.log(nf / max_exact) / math.log(MAX_DISTANCE / max_exact)
                         * (half - max_exact)).astype(jnp.int32)
    large = jnp.minimum(large, half - 1)
    return ret + jnp.where(n < max_exact, n, large)


def to_blocks(a):
    b, s = a.shape[:2]
    a = a.reshape((b, s // QBLOCK, QBLOCK) + a.shape[2:])
    return jnp.moveaxis(a, 1, 0)


def from_blocks(a):
    nblk, b = a.shape[:2]
    a = jnp.moveaxis(a, 0, 1)
    return a.reshape(b, nblk * QBLOCK, -1)


def stick_breaking_attention(q, k, v):
    b, s_len, h, d = q.shape
    scale = 1.0 / math.sqrt(d)
    key_pos = jnp.arange(s_len, dtype=jnp.int32)

    def block(args):
        qb, blk = args
        t = blk * QBLOCK + jnp.arange(QBLOCK, dtype=jnp.int32)
        z = jnp.einsum('bqhd,bshd->bhqs', qb, k).astype(jnp.float32) * scale
        strict = key_pos[None, :] < t[:, None]
        log_rest = jnp.where(strict, jax.nn.log_sigmoid(-z), 0.0)
        tail = lax.cumsum(log_rest, axis=3, reverse=True) - log_rest
        a = jnp.where(strict, jnp.exp(jax.nn.log_sigmoid(z) + tail), 0.0)
        return jnp.einsum('bhqs,bshd->bqhd', a.astype(v.dtype), v)

    nblk = s_len // QBLOCK
    out = lax.map(block, (to_blocks(q), jnp.arange(nblk, dtype=jnp.int32)))
    return from_blocks(out)


def dsa_sparse_attention(q, k, v, q_idx, k_idx, w_idx, rel_bias, topk):
    b, s_len, h, d = q.shape
    scale = 1.0 / math.sqrt(d)
    key_pos = jnp.arange(s_len, dtype=jnp.int32)
    gather = jax.vmap(lambda arr, idx: arr[idx])

    def block(args):
        qb, qib, wib, blk = args
        t = blk * QBLOCK + jnp.arange(QBLOCK, dtype=jnp.int32)
        rel = jax.nn.relu(jnp.einsum('bqhd,bsd->bqhs', qib, k_idx).astype(jnp.float32))
        score = jnp.einsum('bqhs,bqh->bqs', rel, wib.astype(jnp.float32))
        chunk_end = (t // CHUNK + 1) * CHUNK
        admissible = key_pos[None, :] < chunk_end[:, None]
        score = jnp.where(admissible[None], score, -jnp.inf)
        vals, idx = lax.top_k(score, topk)
        valid = vals > -jnp.inf
        kg = gather(k, idx)
        vg = gather(v, idx)
        logits = jnp.einsum('bqhd,bqkhd->bqhk', qb, kg).astype(jnp.float32) * scale
        bias = rel_bias[t5_bucket(idx - t[None, :, None])]
        logits = logits + jnp.swapaxes(bias, -1, -2).astype(jnp.float32)
        logits = jnp.where(valid[:, :, None, :], logits, -jnp.inf)
        p = jax.nn.softmax(logits, axis=-1)
        return jnp.einsum('bqhk,bqkhd->bqhd', p.astype(vg.dtype), vg)

    nblk = s_len // QBLOCK
    out = lax.map(block, (to_blocks(q), to_blocks(q_idx), to_blocks(w_idx),
                          jnp.arange(nblk, dtype=jnp.int32)))
    return from_blocks(out)


def hybrid_layer(x, w_in, w_out, ln_g, ln_b, rel_bias, topk):
    b, s_len, _ = x.shape
    proj = jnp.einsum('bsd,dn->bsn', x, w_in)
    qA, kA, vA, gA, qB, kB, vB, gB, qI, kI, wI = jnp.split(proj, SPLIT_POINTS, axis=-1)
    heads = lambda a, hh: a.reshape(b, s_len, hh, HEAD_DIM)
    oA = stick_breaking_attention(heads(qA, HEADS_A), heads(kA, HEADS_A), heads(vA, HEADS_A))
    oB = dsa_sparse_attention(heads(qB, HEADS_B), heads(kB, HEADS_B), heads(vB, HEADS_B),
                              qI.reshape(b, s_len, IDX_HEADS, IDX_DIM), kI, wI,
                              rel_bias, topk)
    mix = jnp.concatenate([oA * jax.nn.silu(gA), oB * jax.nn.silu(gB)], axis=-1)
    y = jnp.einsum('bsm,md->bsd', mix, w_out)
    return layer_norm(ALPHA * x + y, ln_g, ln_b)


def setup_inputs(seed: int = 0) -> dict:
    key = jax.random.key(seed)
    k_x, k_in, k_out, k_g, k_b, k_rb = jax.random.split(key, 6)
    x = jax.random.normal(k_x, (BATCH, SEQ, D_MODEL), jnp.float32)
    col_scale = np.ones((N_IN,), np.float32)
    for lo, hi in V_COLS:
        col_scale[lo:hi] = DN_BETA
    w_in = (jax.random.normal(k_in, (DEPTH, D_MODEL, N_IN), jnp.float32)
            * (D_MODEL ** -0.5) * jnp.asarray(col_scale))
    w_out = jax.random.normal(k_out, (DEPTH, D_MIX, D_MODEL), jnp.float32) * (D_MIX ** -0.5) * DN_BETA
    ln_g = 1.0 + 0.02 * jax.random.normal(k_g, (DEPTH, D_MODEL), jnp.float32)
    ln_b = 0.02 * jax.random.normal(k_b, (DEPTH, D_MODEL), jnp.float32)
    rel_bias = 0.5 * jax.random.normal(k_rb, (N_BUCKETS, HEADS_B), jnp.float32)
    return {"x": x, "w_in": w_in, "w_out": w_out, "ln_g": ln_g, "ln_b": ln_b,
            "rel_bias": rel_bias}


def reference(x, w_in, w_out, ln_g, ln_b, rel_bias):
    seq_len = x.shape[1]
    topk = min(TOPK_MAX, seq_len // 4)
    h = x
    for layer in range(DEPTH):
        h = hybrid_layer(h, w_in[layer], w_out[layer], ln_g[layer], ln_b[layer], rel_bias, topk)
    return h
```

```python
import functools
import math

import jax
import jax.numpy as jnp
import numpy as np
from jax import lax
from jax.experimental import pallas as pl
from jax.experimental.pallas import tpu as pltpu

HEAD_DIM = 64
HEADS_A = 8
HEADS_B = 8
WIDTH_A = HEADS_A * HEAD_DIM
WIDTH_B = HEADS_B * HEAD_DIM
IDX_HEADS = 8
IDX_DIM = 64
CHUNK = 64
TOPK_MAX = 256
N_BUCKETS = 32
MAX_DISTANCE = 128
LN_EPS = 1e-5

LANES = 128
HEADS_PER_TILE = LANES // HEAD_DIM
VMEM_LIMIT_BYTES = 56 * 1024 * 1024

P_QA, P_KA, P_VA, P_QB, P_KB, P_VB, P_QI = (i * 512 for i in range(7))
P_KI = 7 * 512
P_COLS = P_KI + LANES
G_GA, G_GB, G_WI = 0, 512, 1024
G_COLS = G_WI + LANES

NEG_MASK = -1e30
INT_MIN = -2 ** 31
NEG_INF_KEY = -2139095041

BF16 = jnp.bfloat16
F32 = jnp.float32


def _dot_nt(a, b):
    return lax.dot_general(a, b, (((1,), (1,)), ((), ())), preferred_element_type=F32)


def _dot(a, b):
    return jnp.dot(a, b, preferred_element_type=F32)


def _head_lane_mask(shape, parity):
    lane = lax.broadcasted_iota(jnp.int32, shape, 1)
    return (lane < HEAD_DIM) if parity == 0 else (lane >= HEAD_DIM)


def _proj_kernel(x_ref, w_ref, p_ref, g_ref):
    xb = x_ref[...].astype(BF16)
    for c in range(0, P_COLS, 512):
        cw = min(512, P_COLS - c)
        p_ref[:, c:c + cw] = _dot(xb, w_ref[:, c:c + cw]).astype(BF16)
    for c in range(0, G_COLS, 512):
        cw = min(512, G_COLS - c)
        g_ref[:, c:c + cw] = _dot(xb, w_ref[:, P_COLS + c:P_COLS + c + cw])


def _proj(x2, w_cat, tm):
    m, d = x2.shape
    return pl.pallas_call(
        _proj_kernel,
        out_shape=(jax.ShapeDtypeStruct((m, P_COLS), BF16),
                   jax.ShapeDtypeStruct((m, G_COLS), F32)),
        grid=(m // tm,),
        in_specs=[pl.BlockSpec((tm, d), lambda i: (i, 0)),
                  pl.BlockSpec((d, P_COLS + G_COLS), lambda i: (0, 0))],
        out_specs=(pl.BlockSpec((tm, P_COLS), lambda i: (i, 0)),
                   pl.BlockSpec((tm, G_COLS), lambda i: (i, 0))),
        compiler_params=pltpu.CompilerParams(
            dimension_semantics=("arbitrary",), vmem_limit_bytes=VMEM_LIMIT_BYTES),
        name="proj",
    )(x2, w_cat)


def _softplus(z):
    return jnp.maximum(z, 0.0) + jnp.log(1.0 + jnp.exp(-jnp.abs(z)))


def _sb_block(qh, kblk, vblk, u, strict):
    z = _dot_nt(qh, kblk)
    sp = _softplus(z)
    if strict is not None:
        sp = jnp.where(strict, sp, 0.0)
    hi = sp.astype(BF16)
    lo = (sp - hi.astype(F32)).astype(BF16)
    tail = _dot(jnp.concatenate([hi, lo], axis=1), u)
    a = jnp.exp(z - sp + tail)
    if strict is not None:
        a = jnp.where(strict, a, 0.0)
    pv = _dot(a.astype(BF16), vblk)
    total = tail[:, :1] - sp[:, :1]
    return pv, total


def _sb_kernel(q_ref, k_ref, v_ref, u_ref, g_ref, o_ref, *, tq):
    qi = pl.program_id(2)
    q = q_ref[...]
    u = u_ref[...]
    row = lax.broadcasted_iota(jnp.int32, (tq, tq), 0)
    col = lax.broadcasted_iota(jnp.int32, (tq, tq), 1)
    strict = col < row
    outs = []
    for parity in range(HEADS_PER_TILE):
        qh = jnp.where(_head_lane_mask(q.shape, parity), q, jnp.zeros_like(q))
        start = pl.multiple_of(qi * tq, tq)
        acc, run = _sb_block(qh, k_ref[pl.ds(start, tq), :], v_ref[pl.ds(start, tq), :], u, strict)

        def body(i, carry, qh=qh):
            acc, run = carry
            s0 = pl.multiple_of((qi - 1 - i) * tq, tq)
            pv, total = _sb_block(qh, k_ref[pl.ds(s0, tq), :], v_ref[pl.ds(s0, tq), :], u, None)
            return acc + jnp.exp(run) * pv, run + total

        acc, _ = lax.fori_loop(0, qi, body, (acc, run))
        outs.append(acc)
    o = jnp.where(_head_lane_mask(outs[0].shape, 0), outs[0], outs[1])
    g = g_ref[...]
    o_ref[...] = (o * (g / (1.0 + jnp.exp(-g)))).astype(o_ref.dtype)


def _sb_attention(p3, g3, tq):
    b, s, _ = p3.shape
    tri = np.tril(-np.ones((tq, tq), np.float32), -1)
    u = jnp.asarray(np.concatenate([tri, tri], axis=0), BF16)
    n_pairs = HEADS_A // HEADS_PER_TILE
    qblk, kblk, vblk = P_QA // LANES, P_KA // LANES, P_VA // LANES
    return pl.pallas_call(
        functools.partial(_sb_kernel, tq=tq),
        out_shape=jax.ShapeDtypeStruct((b, s, WIDTH_A), BF16),
        grid=(b, n_pairs, s // tq),
        in_specs=[
            pl.BlockSpec((None, tq, LANES), lambda bi, hp, qi: (bi, qi, qblk + hp)),
            pl.BlockSpec((None, s, LANES), lambda bi, hp, qi: (bi, 0, kblk + hp)),
            pl.BlockSpec((None, s, LANES), lambda bi, hp, qi: (bi, 0, vblk + hp)),
            pl.BlockSpec((2 * tq, tq), lambda bi, hp, qi: (0, 0)),
            pl.BlockSpec((None, tq, LANES), lambda bi, hp, qi: (bi, qi, G_GA // LANES + hp)),
        ],
        out_specs=pl.BlockSpec((None, tq, LANES), lambda bi, hp, qi: (bi, qi, hp)),
        compiler_params=pltpu.CompilerParams(
            dimension_semantics=("arbitrary", "arbitrary", "arbitrary"),
            vmem_limit_bytes=VMEM_LIMIT_BYTES),
        name="sb_attention",
    )(p3, p3, p3, u, g3)


def _sortable_key(x):
    bits = lax.bitcast_convert_type(x, jnp.int32)
    return bits ^ ((bits >> 31) & 0x7FFFFFFF)


def _dsa_kernel(qi_ref, w_ref, ki_ref, q_ref, k_ref, v_ref, bias_ref, g_ref, o_ref, key_ref,
                *, tq, topk, seq):
    qi = pl.program_id(1)
    nblk = qi + 1
    n_slab = tq // LANES
    w = w_ref[...]

    def score_block(j):
        start = pl.multiple_of(j * tq, tq)
        kib = ki_ref[pl.ds(start, tq), :]
        sc = jnp.zeros((tq, tq), F32)
        for h in range(IDX_HEADS):
            qp = qi_ref[:, (h // 2) * LANES:(h // 2 + 1) * LANES]
            qh = jnp.where(_head_lane_mask(qp.shape, h % 2), qp, jnp.zeros_like(qp))
            sc = sc + jnp.maximum(_dot_nt(qh, kib), 0.0) * w[:, h:h + 1]
        return sc

    def p1_body(j, c):
        key_ref[j] = _sortable_key(score_block(j))
        return c

    lax.fori_loop(0, qi, p1_body, 0)
    row = lax.broadcasted_iota(jnp.int32, (tq, tq), 0)
    col = lax.broadcasted_iota(jnp.int32, (tq, tq), 1)
    admissible = col < (row // CHUNK + 1) * CHUNK
    key_ref[qi] = _sortable_key(jnp.where(admissible, score_block(qi), -jnp.inf))

    def count(pred):
        def body(j, acc):
            blk = key_ref[j]
            for t in range(n_slab):
                hit = pred(blk[:, t * LANES:(t + 1) * LANES], j * tq + t * LANES)
                acc = acc + jnp.where(hit, 1.0, 0.0)
            return acc
        acc = lax.fori_loop(0, nblk, body, jnp.zeros((tq, LANES), F32))
        return jnp.sum(acc, axis=1, keepdims=True)

    kf = float(topk)

    def bit_body(i, carry):
        prefix, cnt_ge = carry
        bit = jnp.left_shift(jnp.int32(1), 31 - i)
        cand_u = prefix | bit
        cand = jnp.broadcast_to(cand_u ^ INT_MIN, (tq, LANES))
        cnt = count(lambda slab, c0: slab >= cand)
        take = cnt >= kf
        return jnp.where(take, cand_u, prefix), jnp.where(take, cnt, cnt_ge)

    scanned = (nblk * tq).astype(F32)
    prefix, cnt_ge = lax.fori_loop(
        0, 32, bit_body,
        (jnp.zeros((tq, 1), jnp.int32), jnp.full((tq, 1), 1.0, F32) * scanned))
    thr = prefix ^ INT_MIN
    thr_b = jnp.broadcast_to(thr, (tq, LANES))

    tie = jnp.logical_and(cnt_ge > kf, thr > NEG_INF_KEY)
    has_tie = jnp.max(jnp.where(tie, 1.0, 0.0)) > 0.0
    lane_idx = lax.broadcasted_iota(jnp.int32, (tq, LANES), 1)

    def tie_cut():
        cnt_gt = count(lambda slab, c0: slab > thr_b)
        need = kf - cnt_gt
        nbits = int(seq).bit_length()

        def idx_body(i, lo):
            c_try = lo + jnp.left_shift(jnp.int32(1), nbits - 1 - i)
            c_b = jnp.broadcast_to(c_try, (tq, LANES))
            f = count(lambda slab, c0: jnp.logical_and(slab == thr_b, (lane_idx + c0) < c_b))
            return jnp.where(f < need, c_try, lo)

        lo = lax.fori_loop(0, nbits, idx_body, jnp.zeros((tq, 1), jnp.int32))
        return jnp.where(tie, lo + 1, jnp.int32(2 ** 30))

    cut = lax.cond(has_tie, tie_cut, lambda: jnp.full((tq, 1), 2 ** 30, jnp.int32))
    cut_b = jnp.broadcast_to(cut, (tq, LANES))

    def mask_body(j, c):
        blk = key_ref[j]
        for t in range(n_slab):
            slab = blk[:, t * LANES:(t + 1) * LANES]
            idx = lane_idx + (j * tq + t * LANES)
            tie_sel = jnp.logical_and(slab == thr_b, idx < cut_b)
            sel = jnp.logical_and(jnp.logical_or(slab > thr_b, tie_sel), slab > NEG_INF_KEY)
            m = jnp.where(sel, 0.0, NEG_MASK)
            key_ref[j, :, t * LANES:(t + 1) * LANES] = lax.bitcast_convert_type(m, jnp.int32)
        return c

    lax.fori_loop(0, nblk, mask_body, 0)

    g = g_ref[...]
    for hp in range(HEADS_B // HEADS_PER_TILE):
        qp = q_ref[:, hp * LANES:(hp + 1) * LANES]
        accs = []
        for parity in range(HEADS_PER_TILE):
            h = hp * HEADS_PER_TILE + parity
            qh = jnp.where(_head_lane_mask(qp.shape, parity), qp, jnp.zeros_like(qp))

            def att_body(j, carry, qh=qh, h=h, hp=hp):
                m_run, l_run, acc = carry
                start = pl.multiple_of(j * tq, tq)
                kb = k_ref[pl.ds(start, tq), hp * LANES:(hp + 1) * LANES]
                vb = v_ref[pl.ds(start, tq), hp * LANES:(hp + 1) * LANES]
                d = jnp.minimum(qi - j, 2)
                logits = (_dot_nt(qh, kb) + bias_ref[h, d]
                          + lax.bitcast_convert_type(key_ref[j], F32))
                m_new = jnp.maximum(m_run, jnp.max(logits, axis=1, keepdims=True))
                p = jnp.exp(logits - m_new)
                corr = jnp.exp(m_run - m_new)
                l_new = corr * l_run + jnp.sum(p, axis=1, keepdims=True)
                acc = corr * acc + _dot(p.astype(BF16), vb)
                return m_new, l_new, acc

            init = (jnp.full((tq, 1), NEG_MASK, F32), jnp.zeros((tq, 1), F32),
                    jnp.zeros((tq, LANES), F32))
            _, l_fin, acc = lax.fori_loop(0, nblk, att_body, init)
            accs.append(acc / l_fin)
        o = jnp.where(_head_lane_mask(accs[0].shape, 0), accs[0], accs[1])
        gp = g[:, hp * LANES:(hp + 1) * LANES]
        o_ref[:, hp * LANES:(hp + 1) * LANES] = (o * (gp / (1.0 + jnp.exp(-gp)))).astype(o_ref.dtype)


def _t5_bucket(rel):
    half = N_BUCKETS // 2
    max_exact = half // 2
    ret = jnp.where(rel > 0, half, 0)
    n = jnp.abs(rel)
    nf = jnp.maximum(n, 1).astype(jnp.float32)
    large = max_exact + (jnp.log(nf / max_exact) / math.log(MAX_DISTANCE / max_exact)
                         * (half - max_exact)).astype(jnp.int32)
    large = jnp.minimum(large, half - 1)
    return ret + jnp.where(n < max_exact, n, large)


def _bias_tiles(rel_bias, tq):
    assert tq >= MAX_DISTANCE
    i = jnp.arange(tq, dtype=jnp.int32)[:, None]
    j = jnp.arange(tq, dtype=jnp.int32)[None, :]
    rels = jnp.stack([j - i, j - i - tq, jnp.full((tq, tq), -2 * tq, jnp.int32)])
    far = rel_bias[_t5_bucket(jnp.int32(-2 * tq))]
    tiles = rel_bias[_t5_bucket(rels)] - far
    return jnp.moveaxis(tiles, -1, 0).astype(F32)


def _dsa_attention(p3, g3, rel_bias, tq, topk):
    b, s, _ = p3.shape
    bias = _bias_tiles(rel_bias, tq)
    row = lambda blk: (lambda bi, qi: (bi, qi, blk))
    full = lambda blk: (lambda bi, qi: (bi, 0, blk))
    return pl.pallas_call(
        functools.partial(_dsa_kernel, tq=tq, topk=topk, seq=s),
        out_shape=jax.ShapeDtypeStruct((b, s, WIDTH_B), BF16),
        grid=(b, s // tq),
        in_specs=[
            pl.BlockSpec((None, tq, 512), row(P_QI // 512)),
            pl.BlockSpec((None, tq, LANES), row(G_WI // LANES)),
            pl.BlockSpec((None, s, LANES), full(P_KI // LANES)),
            pl.BlockSpec((None, tq, 512), row(P_QB // 512)),
            pl.BlockSpec((None, s, 512), full(P_KB // 512)),
            pl.BlockSpec((None, s, 512), full(P_VB // 512)),
            pl.BlockSpec((HEADS_B, 3, tq, tq), lambda bi, qi: (0, 0, 0, 0)),
            pl.BlockSpec((None, tq, 512), row(G_GB // 512)),
        ],
        out_specs=pl.BlockSpec((None, tq, WIDTH_B), lambda bi, qi: (bi, qi, 0)),
        scratch_shapes=[pltpu.VMEM((s // tq, tq, tq), jnp.int32)],
        compiler_params=pltpu.CompilerParams(
            dimension_semantics=("arbitrary", "arbitrary"),
            vmem_limit_bytes=VMEM_LIMIT_BYTES),
        name="dsa_attention",
    )(p3, g3, p3, p3, p3, p3, bias, g3)


def _out_kernel(ma_ref, mb_ref, x_ref, w_ref, lg_ref, lb_ref, o_ref, *, alpha):
    y = _dot(ma_ref[...], w_ref[:WIDTH_A, :]) + _dot(mb_ref[...], w_ref[WIDTH_A:, :])
    h = alpha * x_ref[...] + y
    mu = jnp.mean(h, axis=-1, keepdims=True)
    hc = h - mu
    var = jnp.mean(hc * hc, axis=-1, keepdims=True)
    o_ref[...] = hc * lax.rsqrt(var + LN_EPS) * lg_ref[...] + lb_ref[...]


def _out_proj_ln(mix_a, mix_b, x2, w_out, ln_g, ln_b, alpha, tm):
    m, d = x2.shape
    return pl.pallas_call(
        functools.partial(_out_kernel, alpha=alpha),
        out_shape=jax.ShapeDtypeStruct((m, d), F32),
        grid=(m // tm,),
        in_specs=[pl.BlockSpec((tm, WIDTH_A), lambda i: (i, 0)),
                  pl.BlockSpec((tm, WIDTH_B), lambda i: (i, 0)),
                  pl.BlockSpec((tm, d), lambda i: (i, 0)),
                  pl.BlockSpec((WIDTH_A + WIDTH_B, d), lambda i: (0, 0)),
                  pl.BlockSpec((1, d), lambda i: (0, 0)),
                  pl.BlockSpec((1, d), lambda i: (0, 0))],
        out_specs=pl.BlockSpec((tm, d), lambda i: (i, 0)),
        compiler_params=pltpu.CompilerParams(
            dimension_semantics=("arbitrary",), vmem_limit_bytes=VMEM_LIMIT_BYTES),
        name="out_proj_ln",
    )(mix_a, mix_b, x2, w_out, ln_g, ln_b)


def _pack_w_in(w):
    scale = 1.0 / math.sqrt(HEAD_DIM)
    sizes = (WIDTH_A,) * 4 + (WIDTH_B,) * 4 + (IDX_HEADS * IDX_DIM, IDX_DIM, IDX_HEADS)
    qa, ka, va, ga, qb, kb, vb, gb, qi, ki, wi = jnp.split(w, np.cumsum(sizes)[:-1].tolist(), axis=1)
    wi_pad = jnp.pad(wi, ((0, 0), (0, LANES - IDX_HEADS)))
    cat = jnp.concatenate([qa * scale, ka, va, qb * scale, kb, vb, qi, ki, ki, ga, gb, wi_pad], axis=1)
    return cat.astype(BF16)


def _layer(x, w_in, w_out, ln_g, ln_b, rel_bias, alpha, topk):
    b, s, d = x.shape
    x2 = x.reshape(b * s, d)
    p2, g2 = _proj(x2, _pack_w_in(w_in), tm=min(512, b * s))
    p3 = p2.reshape(b, s, P_COLS)
    g3 = g2.reshape(b, s, G_COLS)
    tq = min(256, s)
    mix_a = _sb_attention(p3, g3, tq)
    mix_b = _dsa_attention(p3, g3, rel_bias, tq, topk)
    out = _out_proj_ln(mix_a.reshape(b * s, WIDTH_A), mix_b.reshape(b * s, WIDTH_B), x2,
                       w_out.astype(BF16), ln_g.reshape(1, d), ln_b.reshape(1, d), alpha,
                       tm=min(512, b * s))
    return out.reshape(b, s, d)


def kernel(x, w_in, w_out, ln_g, ln_b, rel_bias):
    depth = w_in.shape[0]
    alpha = (2.0 * depth) ** 0.25
    topk = min(TOPK_MAX, x.shape[1] // 4)
    h = x
    for layer in range(depth):
        h = _layer(h, w_in[layer], w_out[layer], ln_g[layer], ln_b[layer], rel_bias, alpha, topk)
    return h
```

```python
import functools
import math

import jax
import jax.numpy as jnp
import numpy as np
from jax import lax
from jax.experimental import pallas as pl
from jax.experimental.pallas import tpu as pltpu

HEAD_DIM = 64
HEADS_A = 8
HEADS_B = 8
WIDTH_A = HEADS_A * HEAD_DIM
WIDTH_B = HEADS_B * HEAD_DIM
IDX_HEADS = 8
IDX_DIM = 64
CHUNK = 64
TOPK_MAX = 256
N_BUCKETS = 32
MAX_DISTANCE = 128
LN_EPS = 1e-5

LANES = 128
SUBLANES = 8
HEADS_PER_TILE = LANES // HEAD_DIM
VMEM_LIMIT_BYTES = 56 * 1024 * 1024

P_QA, P_KA, P_QB, P_KB, P_QI = (i * 512 for i in range(5))
P_KI = 5 * 512
P_COLS = P_KI + LANES
G_GA, G_GB = 0, 512
G_COLS = 1024
T_VA, T_VB, T_WI = 0, 512, 1024
T_WI_ROWS = 16
T_ROWS = T_WI + T_WI_ROWS

NEG_MASK = -1e30
INT_MIN = -2 ** 31
NEG_INF_KEY = -2139095041
NO_CUT = 2 ** 30
HEAD_GROUP = 4

BF16 = jnp.bfloat16
F32 = jnp.float32


def _dot_nt(a, b):
    return lax.dot_general(a, b, (((1,), (1,)), ((), ())), preferred_element_type=F32)


def _dot(a, b):
    return jnp.dot(a, b, preferred_element_type=F32)


def _mask_head(x, parity):
    lane = lax.broadcasted_iota(jnp.int32, x.shape, 1)
    keep = (lane < HEAD_DIM) if parity == 0 else (lane >= HEAD_DIM)
    return jnp.where(keep, x, jnp.zeros_like(x))


def _silu(g):
    return g / (1.0 + jnp.exp(-g))


def _proj_kernel(x_ref, wr_ref, wt_ref, p_ref, g_ref, vt_ref, wi_ref, *, tk):
    xb = x_ref[...].astype(BF16)
    for c in range(0, P_COLS, 512):
        cw = min(512, P_COLS - c)
        p_ref[:, c:c + cw] = _dot(xb, wr_ref[:, c:c + cw]).astype(BF16)
    for c in range(0, G_COLS, 512):
        g_ref[:, c:c + 512] = _dot(xb, wr_ref[:, P_COLS + c:P_COLS + c + 512])
    for r in range(0, T_WI, 512):
        t = _dot_nt(wt_ref[r:r + 512, :], xb).astype(BF16)
        for kb in range(xb.shape[0] // tk):
            vt_ref[kb, r:r + 512, :] = t[:, kb * tk:(kb + 1) * tk]
    wi_ref[...] = _dot_nt(wt_ref[T_WI:, :], xb)


def _proj(x2, w_row, w_t, tm, tk):
    m, d = x2.shape
    return pl.pallas_call(
        functools.partial(_proj_kernel, tk=tk),
        out_shape=(jax.ShapeDtypeStruct((m, P_COLS), BF16),
                   jax.ShapeDtypeStruct((m, G_COLS), F32),
                   jax.ShapeDtypeStruct((m // tk, T_WI, tk), BF16),
                   jax.ShapeDtypeStruct((T_WI_ROWS, m), F32)),
        grid=(m // tm,),
        in_specs=[pl.BlockSpec((tm, d), lambda i: (i, 0)),
                  pl.BlockSpec((d, P_COLS + G_COLS), lambda i: (0, 0)),
                  pl.BlockSpec((T_ROWS, d), lambda i: (0, 0))],
        out_specs=(pl.BlockSpec((tm, P_COLS), lambda i: (i, 0)),
                   pl.BlockSpec((tm, G_COLS), lambda i: (i, 0)),
                   pl.BlockSpec((tm // tk, T_WI, tk), lambda i: (i, 0, 0)),
                   pl.BlockSpec((T_WI_ROWS, tm), lambda i: (0, i))),
        compiler_params=pltpu.CompilerParams(
            dimension_semantics=("arbitrary",), vmem_limit_bytes=VMEM_LIMIT_BYTES),
        name="proj",
    )(x2, w_row, w_t)


def _softplus(z):
    return jnp.maximum(z, 0.0) + jnp.log(1.0 + jnp.exp(-jnp.abs(z)))


def _sb_block(qhs, kblk, vtblk, u2, strict):
    zts = [_dot_nt(kblk, qh) for qh in qhs]
    sps = [_softplus(zt) for zt in zts]
    if strict is not None:
        sps = [jnp.where(strict, sp, 0.0) for sp in sps]
    his = [sp.astype(BF16) for sp in sps]
    los = [(sp - hi.astype(F32)).astype(BF16) for sp, hi in zip(sps, his)]
    tails = [_dot(u2, jnp.concatenate([hi, lo], axis=0)) for hi, lo in zip(his, los)]
    ws = [jnp.exp(zt - sp + tail) for zt, sp, tail in zip(zts, sps, tails)]
    if strict is not None:
        ws = [jnp.where(strict, a, 0.0) for a in ws]
    pvs = [_dot(vtblk, a.astype(BF16)) for a in ws]
    totals = [tail[:1, :] - sp[:1, :] for tail, sp in zip(tails, sps)]
    return list(zip(pvs, totals))


def _sb_kernel(q_ref, k_ref, vt_ref, u_ref, g_ref, o_ref, *, tq):
    qi = pl.program_id(2)
    q = q_ref[...]
    qh = [_mask_head(q, parity) for parity in range(HEADS_PER_TILE)]
    u2 = u_ref[...]
    key_i = lax.broadcasted_iota(jnp.int32, (tq, tq), 0)
    qry_i = lax.broadcasted_iota(jnp.int32, (tq, tq), 1)
    strict = key_i < qry_i

    def block(j, mask):
        start = pl.multiple_of(j * tq, tq)
        kblk = k_ref[pl.ds(start, tq), :]
        vtblk = vt_ref[j]
        return _sb_block(qh, kblk, vtblk, u2, mask)

    first = block(qi, strict)
    init = tuple(x for pv, total in first for x in (pv, total))

    def body(i, carry):
        res = block(qi - 1 - i, None)
        out = []
        for p in range(HEADS_PER_TILE):
            acc, run = carry[2 * p], carry[2 * p + 1]
            pv, total = res[p]
            out += [acc + jnp.exp(run) * pv, run + total]
        return tuple(out)

    fin = lax.fori_loop(0, qi, body, init)
    row = lax.broadcasted_iota(jnp.int32, fin[0].shape, 0)
    ot = jnp.where(row < HEAD_DIM, fin[0], fin[2])
    o_ref[...] = (ot.T * _silu(g_ref[...])).astype(o_ref.dtype)


def _sb_attention(p3, g3, vt, tq):
    b, s, _ = p3.shape
    nkb = s // tq
    tri = np.triu(-np.ones((tq, tq), np.float32), 1)
    u2 = jnp.asarray(np.concatenate([tri, tri], axis=1), BF16)
    n_pairs = HEADS_A // HEADS_PER_TILE
    return pl.pallas_call(
        functools.partial(_sb_kernel, tq=tq),
        out_shape=jax.ShapeDtypeStruct((b, s, WIDTH_A), BF16),
        grid=(b, n_pairs, nkb),
        in_specs=[
            pl.BlockSpec((None, tq, LANES), lambda bi, hp, qi: (bi, qi, P_QA // LANES + hp)),
            pl.BlockSpec((None, s, LANES), lambda bi, hp, qi: (bi, 0, P_KA // LANES + hp)),
            pl.BlockSpec((nkb, LANES, tq), lambda bi, hp, qi: (bi, T_VA // LANES + hp, 0)),
            pl.BlockSpec((tq, 2 * tq), lambda bi, hp, qi: (0, 0)),
            pl.BlockSpec((None, tq, LANES), lambda bi, hp, qi: (bi, qi, G_GA // LANES + hp)),
        ],
        out_specs=pl.BlockSpec((None, tq, LANES), lambda bi, hp, qi: (bi, qi, hp)),
        compiler_params=pltpu.CompilerParams(
            dimension_semantics=("arbitrary", "arbitrary", "arbitrary"),
            vmem_limit_bytes=VMEM_LIMIT_BYTES),
        name="sb_attention",
    )(p3, p3, vt, u2, g3)


def _sortable_key(x):
    bits = lax.bitcast_convert_type(x, jnp.int32)
    return bits ^ ((bits >> 31) & 0x7FFFFFFF)


def _dsa_kernel(rb_ref, far_ref, qi_ref, wi_ref, ki_ref, q_ref, k_ref, vt_ref, bkt_ref, g_ref,
                o_ref, key_ref, bias_ref, m_ref, l_ref, acc_ref, *, tq, topk, seq):
    qi = pl.program_id(1)
    nblk = qi + 1

    @pl.when(jnp.logical_and(pl.program_id(0) == 0, qi == 0))
    def _():
        far = far_ref[0]
        for h in range(HEADS_B):
            for d in range(2):
                bkt = bkt_ref[d]

                def bucket_body(b, tile, bkt=bkt, h=h):
                    return jnp.where(bkt == b, rb_ref[b, h] - rb_ref[far, h], tile)

                bias_ref[h, d] = lax.fori_loop(0, N_BUCKETS, bucket_body, jnp.zeros((tq, tq), F32))

    wi = wi_ref[...]
    qih = [_mask_head(qi_ref[:, (h // 2) * LANES:(h // 2 + 1) * LANES], h % 2)
           for h in range(IDX_HEADS)]

    def score_block(j):
        start = pl.multiple_of(j * tq, tq)
        kib = ki_ref[pl.ds(start, tq), :]
        sc = jnp.zeros((tq, tq), F32)
        for h in range(IDX_HEADS):
            sc = sc + jnp.maximum(_dot_nt(kib, qih[h]), 0.0) * wi[h:h + 1, :]
        return sc

    def p1_body(j, c):
        key_ref[j] = _sortable_key(score_block(j))
        return c

    lax.fori_loop(0, qi, p1_body, 0)
    key_i = lax.broadcasted_iota(jnp.int32, (tq, tq), 0)
    qry_i = lax.broadcasted_iota(jnp.int32, (tq, tq), 1)
    admissible = key_i < (qry_i // CHUNK + 1) * CHUNK
    key_ref[qi] = _sortable_key(jnp.where(admissible, score_block(qi), -jnp.inf))

    def count(pred):
        def body(j, acc):
            ones = jnp.where(pred(key_ref[j], j * tq), 1.0, 0.0)
            for r in range(0, tq, SUBLANES):
                acc = acc + ones[r:r + SUBLANES, :]
            return acc
        acc = lax.fori_loop(0, nblk, body, jnp.zeros((SUBLANES, tq), F32))
        return jnp.sum(acc, axis=0, keepdims=True)

    kf = float(topk)

    def bit_body(i, carry):
        prefix, cnt_ge = carry
        bit = jnp.left_shift(jnp.int32(1), 31 - i)
        cand_u = prefix | bit
        cand = cand_u ^ INT_MIN
        cnt = count(lambda blk, k0: blk >= cand)
        take = cnt >= kf
        return jnp.where(take, cand_u, prefix), jnp.where(take, cnt, cnt_ge)

    scanned = (nblk * tq).astype(F32)
    prefix, cnt_ge = lax.fori_loop(
        0, 32, bit_body,
        (jnp.zeros((1, tq), jnp.int32), jnp.full((1, tq), 1.0, F32) * scanned))
    thr = prefix ^ INT_MIN

    tie = jnp.logical_and(cnt_ge > kf, thr > NEG_INF_KEY)
    has_tie = jnp.max(jnp.where(tie, 1.0, 0.0)) > 0.0

    def tie_cut():
        cnt_gt = count(lambda blk, k0: blk > thr)
        need = kf - cnt_gt
        nbits = int(seq).bit_length()

        def idx_body(i, lo):
            c_try = lo + jnp.left_shift(jnp.int32(1), nbits - 1 - i)
            f = count(lambda blk, k0: jnp.logical_and(blk == thr, (key_i + k0) < c_try))
            return jnp.where(f < need, c_try, lo)

        lo = lax.fori_loop(0, nbits, idx_body, jnp.zeros((1, tq), jnp.int32))
        return jnp.where(tie, lo + 1, jnp.int32(NO_CUT))

    cut = lax.cond(has_tie, tie_cut, lambda: jnp.full((1, tq), NO_CUT, jnp.int32))

    def mask_body(j, c):
        blk = key_ref[j]
        tie_sel = jnp.logical_and(blk == thr, (key_i + j * tq) < cut)
        sel = jnp.logical_and(jnp.logical_or(blk > thr, tie_sel), blk > NEG_INF_KEY)
        key_ref[j] = lax.bitcast_convert_type(jnp.where(sel, 0.0, NEG_MASK), jnp.int32)
        return c

    lax.fori_loop(0, nblk, mask_body, 0)

    qh = [_mask_head(q_ref[:, (h // 2) * LANES:(h // 2 + 1) * LANES], h % 2)
          for h in range(HEADS_B)]
    m_ref[...] = jnp.full(m_ref.shape, NEG_MASK, F32)
    l_ref[...] = jnp.zeros(l_ref.shape, F32)
    acc_ref[...] = jnp.zeros(acc_ref.shape, F32)

    def att_block(j, d):
        start = pl.multiple_of(j * tq, tq)
        mask = lax.bitcast_convert_type(key_ref[j], F32)
        for h0 in range(0, HEADS_B, HEAD_GROUP):
            hs = range(h0, h0 + HEAD_GROUP)
            pair = lambda h: slice((h // HEADS_PER_TILE) * LANES, (h // HEADS_PER_TILE + 1) * LANES)
            zs = [_dot_nt(k_ref[pl.ds(start, tq), pair(h)], qh[h]) for h in hs]
            logits = [z + mask for z in zs]
            if d is not None:
                logits = [lg + bias_ref[h, d] for lg, h in zip(logits, hs)]
            m_old = [m_ref[h:h + 1, :] for h in hs]
            m_new = [jnp.maximum(mo, jnp.max(lg, axis=0, keepdims=True)) for mo, lg in zip(m_old, logits)]
            ps = [jnp.exp(lg - mn) for lg, mn in zip(logits, m_new)]
            pvs = [_dot(vt_ref[j, pair(h), :], p.astype(BF16)) for p, h in zip(ps, hs)]
            for i, h in enumerate(hs):
                corr = jnp.exp(m_old[i] - m_new[i])
                m_ref[h:h + 1, :] = m_new[i]
                l_ref[h:h + 1, :] = corr * l_ref[h:h + 1, :] + jnp.sum(ps[i], axis=0, keepdims=True)
                acc_ref[h] = corr * acc_ref[h] + pvs[i]

    def far_body(j, c):
        att_block(j, None)
        return c

    lax.fori_loop(0, qi - 1, far_body, 0)

    @pl.when(qi > 0)
    def _():
        att_block(qi - 1, 1)

    att_block(qi, 0)

    row = lax.broadcasted_iota(jnp.int32, (LANES, tq), 0)
    for hp in range(HEADS_B // HEADS_PER_TILE):
        h0 = hp * HEADS_PER_TILE
        o0 = acc_ref[h0] / l_ref[h0:h0 + 1, :]
        o1 = acc_ref[h0 + 1] / l_ref[h0 + 1:h0 + 2, :]
        ot = jnp.where(row < HEAD_DIM, o0, o1)
        gp = g_ref[:, hp * LANES:(hp + 1) * LANES]
        o_ref[:, hp * LANES:(hp + 1) * LANES] = (ot.T * _silu(gp)).astype(o_ref.dtype)


def _t5_bucket(rel):
    half = N_BUCKETS // 2
    max_exact = half // 2
    ret = jnp.where(rel > 0, half, 0)
    n = jnp.abs(rel)
    nf = jnp.maximum(n, 1).astype(jnp.float32)
    large = max_exact + (jnp.log(nf / max_exact) / math.log(MAX_DISTANCE / max_exact)
                         * (half - max_exact)).astype(jnp.int32)
    large = jnp.minimum(large, half - 1)
    return ret + jnp.where(n < max_exact, n, large)


def _dsa_attention(p3, g3, vt, wi_t, rel_bias, tq, topk):
    b, s, _ = p3.shape
    nkb = s // tq
    assert tq >= MAX_DISTANCE
    key = jnp.arange(tq, dtype=jnp.int32)[:, None]
    qry = jnp.arange(tq, dtype=jnp.int32)[None, :]
    buckets = jnp.stack([_t5_bucket(key - qry), _t5_bucket(key - qry - tq)])
    far = _t5_bucket(jnp.full((1,), -tq - 1, jnp.int32))
    row = lambda blk: (lambda bi, qi: (bi, qi, blk))
    full = lambda blk: (lambda bi, qi: (bi, 0, blk))
    smem = pl.BlockSpec(memory_space=pltpu.SMEM)
    return pl.pallas_call(
        functools.partial(_dsa_kernel, tq=tq, topk=topk, seq=s),
        out_shape=jax.ShapeDtypeStruct((b, s, WIDTH_B), BF16),
        grid=(b, nkb),
        in_specs=[
            smem, smem,
            pl.BlockSpec((None, tq, 512), row(P_QI // 512)),
            pl.BlockSpec((T_WI_ROWS, tq), lambda bi, qi: (0, bi * nkb + qi)),
            pl.BlockSpec((None, s, LANES), full(P_KI // LANES)),
            pl.BlockSpec((None, tq, 512), row(P_QB // 512)),
            pl.BlockSpec((None, s, 512), full(P_KB // 512)),
            pl.BlockSpec((nkb, WIDTH_B, tq), lambda bi, qi: (bi, T_VB // WIDTH_B, 0)),
            pl.BlockSpec((2, tq, tq), lambda bi, qi: (0, 0, 0)),
            pl.BlockSpec((None, tq, 512), row(G_GB // 512)),
        ],
        out_specs=pl.BlockSpec((None, tq, WIDTH_B), lambda bi, qi: (bi, qi, 0)),
        scratch_shapes=[pltpu.VMEM((nkb, tq, tq), jnp.int32),
                        pltpu.VMEM((HEADS_B, 2, tq, tq), F32),
                        pltpu.VMEM((HEADS_B, tq), F32),
                        pltpu.VMEM((HEADS_B, tq), F32),
                        pltpu.VMEM((HEADS_B, LANES, tq), F32)],
        compiler_params=pltpu.CompilerParams(
            dimension_semantics=("arbitrary", "arbitrary"),
            vmem_limit_bytes=VMEM_LIMIT_BYTES),
        name="dsa_attention",
    )(rel_bias, far, p3, wi_t, p3, p3, p3, vt, buckets, g3)


def _out_kernel(ma_ref, mb_ref, x_ref, w_ref, lg_ref, lb_ref, o_ref, *, alpha):
    y = _dot(ma_ref[...], w_ref[:WIDTH_A, :]) + _dot(mb_ref[...], w_ref[WIDTH_A:, :])
    h = alpha * x_ref[...] + y
    mu = jnp.mean(h, axis=-1, keepdims=True)
    hc = h - mu
    var = jnp.mean(hc * hc, axis=-1, keepdims=True)
    o_ref[...] = hc * lax.rsqrt(var + LN_EPS) * lg_ref[...] + lb_ref[...]


def _out_proj_ln(mix_a, mix_b, x2, w_out, ln_g, ln_b, alpha, tm):
    m, d = x2.shape
    return pl.pallas_call(
        functools.partial(_out_kernel, alpha=alpha),
        out_shape=jax.ShapeDtypeStruct((m, d), F32),
        grid=(m // tm,),
        in_specs=[pl.BlockSpec((tm, WIDTH_A), lambda i: (i, 0)),
                  pl.BlockSpec((tm, WIDTH_B), lambda i: (i, 0)),
                  pl.BlockSpec((tm, d), lambda i: (i, 0)),
                  pl.BlockSpec((WIDTH_A + WIDTH_B, d), lambda i: (0, 0)),
                  pl.BlockSpec((1, d), lambda i: (0, 0)),
                  pl.BlockSpec((1, d), lambda i: (0, 0))],
        out_specs=pl.BlockSpec((tm, d), lambda i: (i, 0)),
        compiler_params=pltpu.CompilerParams(
            dimension_semantics=("arbitrary",), vmem_limit_bytes=VMEM_LIMIT_BYTES),
        name="out_proj_ln",
    )(mix_a, mix_b, x2, w_out, ln_g, ln_b)


def _pack_w_in(w):
    scale = 1.0 / math.sqrt(HEAD_DIM)
    sizes = (WIDTH_A,) * 4 + (WIDTH_B,) * 4 + (IDX_HEADS * IDX_DIM, IDX_DIM, IDX_HEADS)
    qa, ka, va, ga, qb, kb, vb, gb, qi, ki, wi = jnp.split(w, np.cumsum(sizes)[:-1].tolist(), axis=1)
    w_row = jnp.concatenate([qa * scale, ka, qb * scale, kb, qi, ki, ki, ga, gb], axis=1)
    wi_pad = jnp.pad(wi, ((0, 0), (0, T_WI_ROWS - IDX_HEADS)))
    w_t = jnp.concatenate([va, vb, wi_pad], axis=1).T
    return w_row.astype(BF16), w_t.astype(BF16)


def _layer(x, w_in, w_out, ln_g, ln_b, rel_bias, alpha, topk):
    b, s, d = x.shape
    tq = min(256, s)
    tm = min(512, b * s)
    x2 = x.reshape(b * s, d)
    w_row, w_t = _pack_w_in(w_in)
    p2, g2, vt, wi_t = _proj(x2, w_row, w_t, tm=tm, tk=tq)
    p3 = p2.reshape(b, s, P_COLS)
    g3 = g2.reshape(b, s, G_COLS)
    mix_a = _sb_attention(p3, g3, vt, tq)
    mix_b = _dsa_attention(p3, g3, vt, wi_t, rel_bias, tq, topk)
    out = _out_proj_ln(mix_a.reshape(b * s, WIDTH_A), mix_b.reshape(b * s, WIDTH_B), x2,
                       w_out.astype(BF16), ln_g.reshape(1, d), ln_b.reshape(1, d), alpha, tm=tm)
    return out.reshape(b, s, d)


def kernel(x, w_in, w_out, ln_g, ln_b, rel_bias):
    depth = w_in.shape[0]
    alpha = (2.0 * depth) ** 0.25
    topk = min(TOPK_MAX, x.shape[1] // 4)
    h = x
    for layer in range(depth):
        h = _layer(h, w_in[layer], w_out[layer], ln_g[layer], ln_b[layer], rel_bias, alpha, topk)
    return h
```

```python
import functools
import math

import jax
import jax.numpy as jnp
import numpy as np
from jax import lax
from jax.experimental import pallas as pl
from jax.experimental.pallas import tpu as pltpu

HEAD_DIM = 64
HEADS_A = 8
HEADS_B = 8
WIDTH_A = HEADS_A * HEAD_DIM
WIDTH_B = HEADS_B * HEAD_DIM
IDX_HEADS = 8
IDX_DIM = 64
CHUNK = 64
TOPK_MAX = 256
N_BUCKETS = 32
MAX_DISTANCE = 128
LN_EPS = 1e-5

LANES = 128
SUBLANES = 8
HEADS_PER_TILE = LANES // HEAD_DIM
VMEM_LIMIT_BYTES = 56 * 1024 * 1024

P_QA, P_KA, P_QB, P_KB, P_QI = (i * 512 for i in range(5))
P_KI = 5 * 512
P_COLS = P_KI + LANES
G_GA, G_GB = 0, 512
G_COLS = 1024
T_VA, T_VB, T_WI = 0, 512, 1024
T_WI_ROWS = 16
T_ROWS = T_WI + T_WI_ROWS

NEG_MASK = -1e30
INT_MIN = -2 ** 31
NEG_INF_KEY = -2139095041
NO_CUT = 2 ** 30
HALF16 = 2 ** 15
HEAD_GROUP = 8
LOG_DEAD = -104.0

BF16 = jnp.bfloat16
F32 = jnp.float32


def _dot_nt(a, b):
    return lax.dot_general(a, b, (((1,), (1,)), ((), ())), preferred_element_type=F32)


def _dot(a, b):
    return jnp.dot(a, b, preferred_element_type=F32)


def _mask_head(x, parity):
    lane = lax.broadcasted_iota(jnp.int32, x.shape, 1)
    keep = (lane < HEAD_DIM) if parity == 0 else (lane >= HEAD_DIM)
    return jnp.where(keep, x, jnp.zeros_like(x))


def _silu(g):
    return g / (1.0 + jnp.exp(-g))


def _proj_kernel(x_ref, wr_ref, wt_ref, p_ref, g_ref, vt_ref, wi_ref, *, tk):
    xb = x_ref[...].astype(BF16)
    for c in range(0, P_COLS, 512):
        cw = min(512, P_COLS - c)
        p_ref[:, c:c + cw] = _dot(xb, wr_ref[:, c:c + cw]).astype(BF16)
    for c in range(0, G_COLS, 512):
        g_ref[:, c:c + 512] = _dot(xb, wr_ref[:, P_COLS + c:P_COLS + c + 512])
    for r in range(0, T_WI, 512):
        t = _dot_nt(wt_ref[r:r + 512, :], xb).astype(BF16)
        for kb in range(xb.shape[0] // tk):
            vt_ref[kb, r:r + 512, :] = t[:, kb * tk:(kb + 1) * tk]
    wi_ref[...] = _dot_nt(wt_ref[T_WI:, :], xb)


def _proj(x2, w_row, w_t, tm, tk):
    m, d = x2.shape
    return pl.pallas_call(
        functools.partial(_proj_kernel, tk=tk),
        out_shape=(jax.ShapeDtypeStruct((m, P_COLS), BF16),
                   jax.ShapeDtypeStruct((m, G_COLS), F32),
                   jax.ShapeDtypeStruct((m // tk, T_WI, tk), BF16),
                   jax.ShapeDtypeStruct((T_WI_ROWS, m), F32)),
        grid=(m // tm,),
        in_specs=[pl.BlockSpec((tm, d), lambda i: (i, 0)),
                  pl.BlockSpec((d, P_COLS + G_COLS), lambda i: (0, 0)),
                  pl.BlockSpec((T_ROWS, d), lambda i: (0, 0))],
        out_specs=(pl.BlockSpec((tm, P_COLS), lambda i: (i, 0)),
                   pl.BlockSpec((tm, G_COLS), lambda i: (i, 0)),
                   pl.BlockSpec((tm // tk, T_WI, tk), lambda i: (i, 0, 0)),
                   pl.BlockSpec((T_WI_ROWS, tm), lambda i: (0, i))),
        compiler_params=pltpu.CompilerParams(
            dimension_semantics=("arbitrary",), vmem_limit_bytes=VMEM_LIMIT_BYTES),
        name="proj",
    )(x2, w_row, w_t)


def _softplus(z):
    return jnp.maximum(z, 0.0) + jnp.log(1.0 + jnp.exp(-jnp.abs(z)))


def _sb_blocks(chains, u2):
    stricts = [c[3] for c in chains]
    masked = lambda xs: [x if m is None else jnp.where(m, x, 0.0) for x, m in zip(xs, stricts)]
    zts = [_dot_nt(kblk, qh) for qh, kblk, _, _ in chains]
    sps = masked([_softplus(zt) for zt in zts])
    his = [sp.astype(BF16) for sp in sps]
    los = [(sp - hi.astype(F32)).astype(BF16) for sp, hi in zip(sps, his)]
    tails = [_dot(u2, jnp.concatenate([hi, lo], axis=0)) for hi, lo in zip(his, los)]
    ws = masked([jnp.exp(zt - sp + tail) for zt, sp, tail in zip(zts, sps, tails)])
    pvs = [_dot(c[2], a.astype(BF16)) for c, a in zip(chains, ws)]
    totals = [tail[:1, :] - sp[:1, :] for tail, sp in zip(tails, sps)]
    return list(zip(pvs, totals))


def _sb_kernel(q_ref, k_ref, vt_ref, u_ref, g_ref, o_ref, *, tq):
    qi = pl.program_id(2)
    q = q_ref[...]
    qh = [_mask_head(q, parity) for parity in range(HEADS_PER_TILE)]
    u2 = u_ref[...]
    key_i = lax.broadcasted_iota(jnp.int32, (tq, tq), 0)
    qry_i = lax.broadcasted_iota(jnp.int32, (tq, tq), 1)
    strict = key_i < qry_i

    def chains(j, mask):
        start = pl.multiple_of(j * tq, tq)
        kblk = k_ref[pl.ds(start, tq), :]
        vtblk = vt_ref[j]
        return [(qh[p], kblk, vtblk, mask) for p in range(HEADS_PER_TILE)]

    has_prev = qi > 0
    res = _sb_blocks(chains(qi, strict) + chains(jnp.maximum(qi - 1, 0), None), u2)
    state = []
    for p in range(HEADS_PER_TILE):
        (pv_d, tot_d), (pv_p, tot_p) = res[p], res[HEADS_PER_TILE + p]
        state += [pv_d + jnp.where(has_prev, jnp.exp(tot_d), 0.0) * pv_p, tot_d + tot_p]

    def alive_after(runs):
        top = jnp.max(jnp.maximum(runs[0], runs[1]))
        return top > LOG_DEAD

    def cond(carry):
        j, alive = carry[0], carry[1]
        return jnp.logical_and(j >= 0, alive)

    def body(carry):
        j, st = carry[0], carry[2:]
        res = _sb_blocks(chains(j, None), u2)
        out = []
        for p in range(HEADS_PER_TILE):
            acc, run = st[2 * p], st[2 * p + 1]
            pv, total = res[p]
            out += [acc + jnp.exp(run) * pv, run + total]
        return (j - 1, alive_after([out[1], out[3]])) + tuple(out)

    fin = lax.while_loop(cond, body, (qi - 2, alive_after([state[1], state[3]])) + tuple(state))[2:]
    row = lax.broadcasted_iota(jnp.int32, fin[0].shape, 0)
    ot = jnp.where(row < HEAD_DIM, fin[0], fin[2])
    o_ref[...] = (ot.T * _silu(g_ref[...])).astype(o_ref.dtype)


def _sb_attention(p3, g3, vt, tq):
    b, s, _ = p3.shape
    nkb = s // tq
    tri = np.triu(-np.ones((tq, tq), np.float32), 1)
    u2 = jnp.asarray(np.concatenate([tri, tri], axis=1), BF16)
    n_pairs = HEADS_A // HEADS_PER_TILE
    return pl.pallas_call(
        functools.partial(_sb_kernel, tq=tq),
        out_shape=jax.ShapeDtypeStruct((b, s, WIDTH_A), BF16),
        grid=(b, n_pairs, nkb),
        in_specs=[
            pl.BlockSpec((None, tq, LANES), lambda bi, hp, qi: (bi, qi, P_QA // LANES + hp)),
            pl.BlockSpec((None, s, LANES), lambda bi, hp, qi: (bi, 0, P_KA // LANES + hp)),
            pl.BlockSpec((nkb, LANES, tq), lambda bi, hp, qi: (bi, T_VA // LANES + hp, 0)),
            pl.BlockSpec((tq, 2 * tq), lambda bi, hp, qi: (0, 0)),
            pl.BlockSpec((None, tq, LANES), lambda bi, hp, qi: (bi, qi, G_GA // LANES + hp)),
        ],
        out_specs=pl.BlockSpec((None, tq, LANES), lambda bi, hp, qi: (bi, qi, hp)),
        compiler_params=pltpu.CompilerParams(
            dimension_semantics=("arbitrary", "arbitrary", "arbitrary"),
            vmem_limit_bytes=VMEM_LIMIT_BYTES),
        name="sb_attention",
    )(p3, p3, vt, u2, g3)


def _sortable_key(x):
    bits = lax.bitcast_convert_type(x, jnp.int32)
    return bits ^ ((bits >> 31) & 0x7FFFFFFF)


def _dsa_kernel(rb_ref, far_ref, qi_ref, wi_ref, ki_ref, q_ref, k_ref, vt_ref, bkt_ref, g_ref,
                o_ref, key_ref, hi_ref, lo_ref, bias_ref, m_ref, acc_ref, *, tq, topk, seq):
    qi = pl.program_id(1)
    nblk = qi + 1

    @pl.when(jnp.logical_and(pl.program_id(0) == 0, qi == 0))
    def _():
        far = far_ref[0]
        for h in range(HEADS_B):
            for d in range(2):
                bkt = bkt_ref[d]

                def bucket_body(b, tile, bkt=bkt, h=h):
                    return jnp.where(bkt == b, rb_ref[b, h] - rb_ref[far, h], tile)

                bias_ref[h, d] = lax.fori_loop(0, N_BUCKETS, bucket_body, jnp.zeros((tq, tq), F32))

    wi = wi_ref[...]
    qih = [_mask_head(qi_ref[:, (h // 2) * LANES:(h // 2 + 1) * LANES], h % 2)
           for h in range(IDX_HEADS)]

    def score_block(j):
        start = pl.multiple_of(j * tq, tq)
        kib = ki_ref[pl.ds(start, tq), :]
        sc = jnp.zeros((tq, tq), F32)
        for h in range(IDX_HEADS):
            sc = sc + jnp.maximum(_dot_nt(kib, qih[h]), 0.0) * wi[h:h + 1, :]
        return sc

    def store_keys(j, sc):
        key = _sortable_key(sc)
        key_ref[j] = key
        hi_ref[j] = (key >> 16).astype(jnp.int16)

    def p1_body(j, c):
        store_keys(j, score_block(j))
        return c

    lax.fori_loop(0, qi, p1_body, 0)
    key_i = lax.broadcasted_iota(jnp.int32, (tq, tq), 0)
    qry_i = lax.broadcasted_iota(jnp.int32, (tq, tq), 1)
    admissible = key_i < (qry_i // CHUNK + 1) * CHUNK
    store_keys(qi, jnp.where(admissible, score_block(qi), -jnp.inf))

    def count(pred):
        def body(j, acc):
            ones = jnp.where(pred(key_ref[j], j * tq), 1.0, 0.0)
            for r in range(0, tq, SUBLANES):
                acc = acc + ones[r:r + SUBLANES, :]
            return acc
        acc = lax.fori_loop(0, nblk, body, jnp.zeros((SUBLANES, tq), F32))
        return jnp.sum(acc, axis=0, keepdims=True)

    def count16(ref, pred):
        rows = 2 * SUBLANES
        def body(j, acc):
            ones = jnp.where(pred(ref[j]), jnp.int16(1), jnp.int16(0))
            for r in range(0, tq, rows):
                acc = acc + ones[r:r + rows, :]
            return acc
        acc = lax.fori_loop(0, nblk, body, jnp.zeros((rows, tq), jnp.int16))
        return jnp.sum(acc.astype(F32), axis=0, keepdims=True)

    def search16(ref, target):
        def bit_body(i, prefix):
            cand_u = prefix | jnp.left_shift(jnp.int32(1), 15 - i)
            cand = (cand_u - HALF16).astype(jnp.int16)
            cnt = count16(ref, lambda blk: blk >= cand)
            return jnp.where(cnt >= target, cand_u, prefix)
        return lax.fori_loop(0, 16, bit_body, jnp.zeros((1, tq), jnp.int32)) - HALF16

    kf = float(topk)
    assert seq // (2 * SUBLANES) < 2 ** 15
    thr_hi = search16(hi_ref, kf)
    thr_hi16 = thr_hi.astype(jnp.int16)
    need_lo = kf - count16(hi_ref, lambda blk: blk > thr_hi16)

    def low_body(j, c):
        lo = ((key_ref[j] & 0xFFFF) - HALF16).astype(jnp.int16)
        lo_ref[j] = jnp.where(hi_ref[j] == thr_hi16, lo, jnp.int16(-HALF16))
        return c

    lax.fori_loop(0, nblk, low_body, 0)
    thr_lo = search16(lo_ref, need_lo)
    thr = thr_hi * (2 * HALF16) + (thr_lo + HALF16)
    cnt_ge = count(lambda blk, k0: blk >= thr)

    tie = jnp.logical_and(cnt_ge > kf, thr > NEG_INF_KEY)
    has_tie = jnp.max(jnp.where(tie, 1.0, 0.0)) > 0.0

    def tie_cut():
        cnt_gt = count(lambda blk, k0: blk > thr)
        need = kf - cnt_gt
        nbits = int(seq).bit_length()

        def idx_body(i, lo):
            c_try = lo + jnp.left_shift(jnp.int32(1), nbits - 1 - i)
            f = count(lambda blk, k0: jnp.logical_and(blk == thr, (key_i + k0) < c_try))
            return jnp.where(f < need, c_try, lo)

        lo = lax.fori_loop(0, nbits, idx_body, jnp.zeros((1, tq), jnp.int32))
        return jnp.where(tie, lo + 1, jnp.int32(NO_CUT))

    cut = lax.cond(has_tie, tie_cut, lambda: jnp.full((1, tq), NO_CUT, jnp.int32))

    def mask_body(j, c):
        blk = key_ref[j]
        tie_sel = jnp.logical_and(blk == thr, (key_i + j * tq) < cut)
        sel = jnp.logical_and(jnp.logical_or(blk > thr, tie_sel), blk > NEG_INF_KEY)
        key_ref[j] = lax.bitcast_convert_type(jnp.where(sel, 0.0, NEG_MASK), jnp.int32)
        return c

    lax.fori_loop(0, nblk, mask_body, 0)

    qh = [_mask_head(q_ref[:, (h // 2) * LANES:(h // 2 + 1) * LANES], h % 2)
          for h in range(HEADS_B)]
    m_ref[...] = jnp.full(m_ref.shape, NEG_MASK, F32)
    acc_ref[...] = jnp.zeros(acc_ref.shape, F32)
    vrow = lax.broadcasted_iota(jnp.int32, (LANES, tq), 0)

    def att_block(j, nb, d):
        start = pl.multiple_of(j * tq, tq)
        cat = lambda xs, axis: xs[0] if nb == 1 else jnp.concatenate(xs, axis=axis)
        mask = lax.bitcast_convert_type(cat([key_ref[j + i] for i in range(nb)], 0), F32)
        vts = []
        for hp in range(HEADS_B // HEADS_PER_TILE):
            vt = cat([vt_ref[j + i, hp * LANES:(hp + 1) * LANES, :] for i in range(nb)], 1)
            one = jnp.ones_like(vt)
            head0_rows = lax.broadcasted_iota(jnp.int32, vt.shape, 0) < HEAD_DIM
            vts.append([jnp.where(head0_rows, vt, one), jnp.where(head0_rows, one, vt)])
        for h0 in range(0, HEADS_B, HEAD_GROUP):
            hs = range(h0, h0 + HEAD_GROUP)
            pair = lambda h: slice((h // HEADS_PER_TILE) * LANES, (h // HEADS_PER_TILE + 1) * LANES)
            zs = [_dot_nt(k_ref[pl.ds(start, tq * nb), pair(h)], qh[h]) for h in hs]
            logits = [z + mask for z in zs]
            if d is not None:
                logits = [lg + bias_ref[h, d] for lg, h in zip(logits, hs)]
            m_old = [m_ref[h:h + 1, :] for h in hs]
            m_new = [jnp.maximum(mo, jnp.max(lg, axis=0, keepdims=True)) for mo, lg in zip(m_old, logits)]
            ps = [jnp.exp(lg - mn) for lg, mn in zip(logits, m_new)]
            pvs = [_dot(vts[h // HEADS_PER_TILE][h % HEADS_PER_TILE], p.astype(BF16)) for p, h in zip(ps, hs)]
            for i, h in enumerate(hs):
                m_ref[h:h + 1, :] = m_new[i]
                acc_ref[h] = jnp.exp(m_old[i] - m_new[i]) * acc_ref[h] + pvs[i]

    n_far = qi - 1

    def far_body(i, c):
        att_block(2 * i, 2, None)
        return c

    lax.fori_loop(0, n_far // 2, far_body, 0)

    @pl.when(jnp.logical_and(n_far > 0, n_far % 2 == 1))
    def _():
        att_block(n_far - 1, 1, None)

    @pl.when(qi > 0)
    def _():
        att_block(qi - 1, 1, 1)

    att_block(qi, 1, 0)

    for hp in range(HEADS_B // HEADS_PER_TILE):
        h0 = hp * HEADS_PER_TILE
        a0, a1 = acc_ref[h0], acc_ref[h0 + 1]
        o0 = a0 / a0[LANES - 1:LANES, :]
        o1 = a1 / a1[0:1, :]
        ot = jnp.where(vrow < HEAD_DIM, o0, o1)
        gp = g_ref[:, hp * LANES:(hp + 1) * LANES]
        o_ref[:, hp * LANES:(hp + 1) * LANES] = (ot.T * _silu(gp)).astype(o_ref.dtype)


def _t5_bucket(rel):
    half = N_BUCKETS // 2
    max_exact = half // 2
    ret = jnp.where(rel > 0, half, 0)
    n = jnp.abs(rel)
    nf = jnp.maximum(n, 1).astype(jnp.float32)
    large = max_exact + (jnp.log(nf / max_exact) / math.log(MAX_DISTANCE / max_exact)
                         * (half - max_exact)).astype(jnp.int32)
    large = jnp.minimum(large, half - 1)
    return ret + jnp.where(n < max_exact, n, large)


def _dsa_attention(p3, g3, vt, wi_t, rel_bias, tq, topk):
    b, s, _ = p3.shape
    nkb = s // tq
    assert tq >= MAX_DISTANCE
    key = jnp.arange(tq, dtype=jnp.int32)[:, None]
    qry = jnp.arange(tq, dtype=jnp.int32)[None, :]
    buckets = jnp.stack([_t5_bucket(key - qry), _t5_bucket(key - qry - tq)])
    far = _t5_bucket(jnp.full((1,), -tq - 1, jnp.int32))
    row = lambda blk: (lambda bi, qi: (bi, qi, blk))
    full = lambda blk: (lambda bi, qi: (bi, 0, blk))
    smem = pl.BlockSpec(memory_space=pltpu.SMEM)
    return pl.pallas_call(
        functools.partial(_dsa_kernel, tq=tq, topk=topk, seq=s),
        out_shape=jax.ShapeDtypeStruct((b, s, WIDTH_B), BF16),
        grid=(b, nkb),
        in_specs=[
            smem, smem,
            pl.BlockSpec((None, tq, 512), row(P_QI // 512)),
            pl.BlockSpec((T_WI_ROWS, tq), lambda bi, qi: (0, bi * nkb + qi)),
            pl.BlockSpec((None, s, LANES), full(P_KI // LANES)),
            pl.BlockSpec((None, tq, 512), row(P_QB // 512)),
            pl.BlockSpec((None, s, 512), full(P_KB // 512)),
            pl.BlockSpec((nkb, WIDTH_B, tq), lambda bi, qi: (bi, T_VB // WIDTH_B, 0)),
            pl.BlockSpec((2, tq, tq), lambda bi, qi: (0, 0, 0)),
            pl.BlockSpec((None, tq, 512), row(G_GB // 512)),
        ],
        out_specs=pl.BlockSpec((None, tq, WIDTH_B), lambda bi, qi: (bi, qi, 0)),
        scratch_shapes=[pltpu.VMEM((nkb, tq, tq), jnp.int32),
                        pltpu.VMEM((nkb, tq, tq), jnp.int16),
                        pltpu.VMEM((nkb, tq, tq), jnp.int16),
                        pltpu.VMEM((HEADS_B, 2, tq, tq), F32),
                        pltpu.VMEM((HEADS_B, tq), F32),
                        pltpu.VMEM((HEADS_B, LANES, tq), F32)],
        compiler_params=pltpu.CompilerParams(
            dimension_semantics=("arbitrary", "arbitrary"),
            vmem_limit_bytes=VMEM_LIMIT_BYTES),
        name="dsa_attention",
    )(rel_bias, far, p3, wi_t, p3, p3, p3, vt, buckets, g3)


def _out_kernel(ma_ref, mb_ref, x_ref, w_ref, lg_ref, lb_ref, o_ref, *, alpha):
    y = _dot(ma_ref[...], w_ref[:WIDTH_A, :]) + _dot(mb_ref[...], w_ref[WIDTH_A:, :])
    h = alpha * x_ref[...] + y
    mu = jnp.mean(h, axis=-1, keepdims=True)
    hc = h - mu
    var = jnp.mean(hc * hc, axis=-1, keepdims=True)
    o_ref[...] = hc * lax.rsqrt(var + LN_EPS) * lg_ref[...] + lb_ref[...]


def _out_proj_ln(mix_a, mix_b, x2, w_out, ln_g, ln_b, alpha, tm):
    m, d = x2.shape
    return pl.pallas_call(
        functools.partial(_out_kernel, alpha=alpha),
        out_shape=jax.ShapeDtypeStruct((m, d), F32),
        grid=(m // tm,),
        in_specs=[pl.BlockSpec((tm, WIDTH_A), lambda i: (i, 0)),
                  pl.BlockSpec((tm, WIDTH_B), lambda i: (i, 0)),
                  pl.BlockSpec((tm, d), lambda i: (i, 0)),
                  pl.BlockSpec((WIDTH_A + WIDTH_B, d), lambda i: (0, 0)),
                  pl.BlockSpec((1, d), lambda i: (0, 0)),
                  pl.BlockSpec((1, d), lambda i: (0, 0))],
        out_specs=pl.BlockSpec((tm, d), lambda i: (i, 0)),
        compiler_params=pltpu.CompilerParams(
            dimension_semantics=("arbitrary",), vmem_limit_bytes=VMEM_LIMIT_BYTES),
        name="out_proj_ln",
    )(mix_a, mix_b, x2, w_out, ln_g, ln_b)


def _pack_w_in(w):
    scale = 1.0 / math.sqrt(HEAD_DIM)
    sizes = (WIDTH_A,) * 4 + (WIDTH_B,) * 4 + (IDX_HEADS * IDX_DIM, IDX_DIM, IDX_HEADS)
    qa, ka, va, ga, qb, kb, vb, gb, qi, ki, wi = jnp.split(w, np.cumsum(sizes)[:-1].tolist(), axis=1)
    w_row = jnp.concatenate([qa * scale, ka, qb * scale, kb, qi, ki, ki, ga, gb], axis=1)
    wi_pad = jnp.pad(wi, ((0, 0), (0, T_WI_ROWS - IDX_HEADS)))
    w_t = jnp.concatenate([va, vb, wi_pad], axis=1).T
    return w_row.astype(BF16), w_t.astype(BF16)


def _layer(x, w_in, w_out, ln_g, ln_b, rel_bias, alpha, topk):
    b, s, d = x.shape
    tq = min(256, s)
    tm = min(512, b * s)
    x2 = x.reshape(b * s, d)
    w_row, w_t = _pack_w_in(w_in)
    p2, g2, vt, wi_t = _proj(x2, w_row, w_t, tm=tm, tk=tq)
    p3 = p2.reshape(b, s, P_COLS)
    g3 = g2.reshape(b, s, G_COLS)
    mix_a = _sb_attention(p3, g3, vt, tq)
    mix_b = _dsa_attention(p3, g3, vt, wi_t, rel_bias, tq, topk)
    out = _out_proj_ln(mix_a.reshape(b * s, WIDTH_A), mix_b.reshape(b * s, WIDTH_B), x2,
                       w_out.astype(BF16), ln_g.reshape(1, d), ln_b.reshape(1, d), alpha, tm=tm)
    return out.reshape(b, s, d)


def kernel(x, w_in, w_out, ln_g, ln_b, rel_bias):
    depth = w_in.shape[0]
    alpha = (2.0 * depth) ** 0.25
    topk = min(TOPK_MAX, x.shape[1] // 4)
    h = x
    for layer in range(depth):
        h = _layer(h, w_in[layer], w_out[layer], ln_g[layer], ln_b[layer], rel_bias, alpha, topk)
    return h
```

```python
import functools
import math

import jax
import jax.numpy as jnp
import numpy as np
from jax import lax
from jax.experimental import pallas as pl
from jax.experimental.pallas import tpu as pltpu

HEAD_DIM = 64
HEADS_A = 8
HEADS_B = 8
WIDTH_A = HEADS_A * HEAD_DIM
WIDTH_B = HEADS_B * HEAD_DIM
IDX_HEADS = 8
IDX_DIM = 64
CHUNK = 64
TOPK_MAX = 256
N_BUCKETS = 32
MAX_DISTANCE = 128
LN_EPS = 1e-5

LANES = 128
SUBLANES = 8
HEADS_PER_TILE = LANES // HEAD_DIM
VMEM_LIMIT_BYTES = 56 * 1024 * 1024

P_QA, P_KA, P_QB, P_KB, P_QI = (i * 512 for i in range(5))
P_KI = 5 * 512
P_COLS = P_KI + LANES
G_GA, G_GB = 0, 512
G_COLS = 1024
T_VA, T_VB, T_WI = 0, 512, 1024
T_WI_ROWS = 16
T_ROWS = T_WI + T_WI_ROWS

NEG_MASK = -1e30
INT_MIN = -2 ** 31
NEG_INF_KEY = -2139095041
NO_CUT = 2 ** 30
HALF16 = 2 ** 15
SOFTMAX_SLACK = 40.0
LOG2E = math.log2(math.e)
SB_HEADS_PER_STEP = 8
LOG2_DEAD = -151.0

BF16 = jnp.bfloat16
F32 = jnp.float32


def _dot_nt(a, b):
    return lax.dot_general(a, b, (((1,), (1,)), ((), ())), preferred_element_type=F32)


def _dot(a, b):
    return jnp.dot(a, b, preferred_element_type=F32)


def _mask_head(x, parity):
    lane = lax.broadcasted_iota(jnp.int32, x.shape, 1)
    keep = (lane < HEAD_DIM) if parity == 0 else (lane >= HEAD_DIM)
    return jnp.where(keep, x, jnp.zeros_like(x))


def _silu(g):
    return g / (1.0 + jnp.exp(-g))


def _proj_kernel(x_ref, wr_ref, wt_ref, p_ref, g_ref, vt_ref, wi_ref, *, tk):
    xb = x_ref[...].astype(BF16)
    for c in range(0, P_COLS, 512):
        cw = min(512, P_COLS - c)
        p_ref[:, c:c + cw] = _dot(xb, wr_ref[:, c:c + cw]).astype(BF16)
    for c in range(0, G_COLS, 512):
        g_ref[:, c:c + 512] = _dot(xb, wr_ref[:, P_COLS + c:P_COLS + c + 512])
    for r in range(0, T_WI, 512):
        t = _dot_nt(wt_ref[r:r + 512, :], xb).astype(BF16)
        for kb in range(xb.shape[0] // tk):
            vt_ref[kb, r:r + 512, :] = t[:, kb * tk:(kb + 1) * tk]
    wi_ref[...] = _dot_nt(wt_ref[T_WI:, :], xb)


def _proj(x2, w_row, w_t, tm, tk):
    m, d = x2.shape
    return pl.pallas_call(
        functools.partial(_proj_kernel, tk=tk),
        out_shape=(jax.ShapeDtypeStruct((m, P_COLS), BF16),
                   jax.ShapeDtypeStruct((m, G_COLS), F32),
                   jax.ShapeDtypeStruct((m // tk, T_WI, tk), BF16),
                   jax.ShapeDtypeStruct((T_WI_ROWS, m), F32)),
        grid=(m // tm,),
        in_specs=[pl.BlockSpec((tm, d), lambda i: (i, 0)),
                  pl.BlockSpec((d, P_COLS + G_COLS), lambda i: (0, 0)),
                  pl.BlockSpec((T_ROWS, d), lambda i: (0, 0))],
        out_specs=(pl.BlockSpec((tm, P_COLS), lambda i: (i, 0)),
                   pl.BlockSpec((tm, G_COLS), lambda i: (i, 0)),
                   pl.BlockSpec((tm // tk, T_WI, tk), lambda i: (i, 0, 0)),
                   pl.BlockSpec((T_WI_ROWS, tm), lambda i: (0, i))),
        compiler_params=pltpu.CompilerParams(
            dimension_semantics=("arbitrary",), vmem_limit_bytes=VMEM_LIMIT_BYTES),
        name="proj",
    )(x2, w_row, w_t)


def _softplus2(z):
    return jnp.maximum(z, 0.0) + jnp.log2(1.0 + jnp.exp2(-jnp.abs(z)))


def _sb_blocks(chains, u2):
    half = u2.shape[0]
    stricts = [c[3] for c in chains]
    masked = lambda xs: [x if m is None else jnp.where(m, x, 0.0) for x, m in zip(xs, stricts)]
    zts = [_dot_nt(kblk, qh) for qh, kblk, _, _ in chains]
    sps = masked([_softplus2(zt) for zt in zts])
    his = [sp.astype(BF16) for sp in sps]
    los = [(sp - hi.astype(F32)).astype(BF16) for sp, hi in zip(sps, his)]
    halves = lambda x: (x[:half], x[half:])
    tails = []
    for sp, hi, lo in zip(sps, his, los):
        (hi_a, hi_b), (lo_a, lo_b) = halves(hi), halves(lo)
        tail_a = _dot(u2, jnp.concatenate([hi_a, lo_a], axis=0))
        tail_b = _dot(u2, jnp.concatenate([hi_b, lo_b], axis=0))
        total_b = tail_b[:1, :] - sp[half:half + 1, :]
        tails.append(jnp.concatenate([tail_a + total_b, tail_b], axis=0))
    ws = masked([jnp.exp2(zt - sp + tail) for zt, sp, tail in zip(zts, sps, tails)])
    pvs = [_dot(c[2], a.astype(BF16)) for c, a in zip(chains, ws)]
    totals = [tail[:1, :] - sp[:1, :] for tail, sp in zip(tails, sps)]
    return list(zip(pvs, totals))


def _sb_kernel(q_ref, k_ref, vt_ref, u_ref, g_ref, o_ref, *, tq, n_heads):
    qi = pl.program_id(2)
    pair = lambda h: slice((h // HEADS_PER_TILE) * LANES, (h // HEADS_PER_TILE + 1) * LANES)
    qh = [_mask_head(q_ref[:, pair(h)], h % HEADS_PER_TILE) for h in range(n_heads)]
    u2 = u_ref[...]
    key_i = lax.broadcasted_iota(jnp.int32, (tq, tq), 0)
    qry_i = lax.broadcasted_iota(jnp.int32, (tq, tq), 1)
    strict = key_i < qry_i

    def chains(j, mask):
        start = pl.multiple_of(j * tq, tq)
        return [(qh[h], k_ref[pl.ds(start, tq), pair(h)], vt_ref[j, pair(h), :], mask)
                for h in range(n_heads)]

    has_prev = qi > 0
    res = _sb_blocks(chains(qi, strict) + chains(jnp.maximum(qi - 1, 0), None), u2)
    state = []
    for h in range(n_heads):
        (pv_d, tot_d), (pv_p, tot_p) = res[h], res[n_heads + h]
        state += [pv_d + jnp.where(has_prev, jnp.exp2(tot_d), 0.0) * pv_p, tot_d + tot_p]

    def alive_after(st):
        top = functools.reduce(jnp.maximum, st[1::2])
        return jnp.max(top) > LOG2_DEAD

    def cond(carry):
        j, alive = carry[0], carry[1]
        return jnp.logical_and(j >= 0, alive)

    def body(carry):
        j, st = carry[0], carry[2:]
        res = _sb_blocks(chains(j, None), u2)
        out = []
        for h in range(n_heads):
            acc, run = st[2 * h], st[2 * h + 1]
            pv, total = res[h]
            out += [acc + jnp.exp2(run) * pv, run + total]
        return (j - 1, alive_after(out)) + tuple(out)

    fin = lax.while_loop(cond, body, (qi - 2, alive_after(state)) + tuple(state))[2:]
    row = lax.broadcasted_iota(jnp.int32, fin[0].shape, 0)
    for hp in range(n_heads // HEADS_PER_TILE):
        acc0, acc1 = fin[2 * (2 * hp)], fin[2 * (2 * hp + 1)]
        ot = jnp.where(row < HEAD_DIM, acc0, acc1)
        o_ref[:, pair(2 * hp)] = (ot.T * _silu(g_ref[:, pair(2 * hp)])).astype(o_ref.dtype)


def _sb_attention(p3, g3, vt, tq):
    b, s, _ = p3.shape
    nkb = s // tq
    half = tq // 2
    tri = np.triu(-np.ones((half, half), np.float32), 1)
    u2 = jnp.asarray(np.concatenate([tri, tri], axis=1), BF16)
    w = SB_HEADS_PER_STEP * HEAD_DIM
    return pl.pallas_call(
        functools.partial(_sb_kernel, tq=tq, n_heads=SB_HEADS_PER_STEP),
        out_shape=jax.ShapeDtypeStruct((b, s, WIDTH_A), BF16),
        grid=(b, HEADS_A // SB_HEADS_PER_STEP, nkb),
        in_specs=[
            pl.BlockSpec((None, tq, w), lambda bi, hp, qi: (bi, qi, P_QA // w + hp)),
            pl.BlockSpec((None, s, w), lambda bi, hp, qi: (bi, 0, P_KA // w + hp)),
            pl.BlockSpec((nkb, w, tq), lambda bi, hp, qi: (bi, T_VA // w + hp, 0)),
            pl.BlockSpec((half, 2 * half), lambda bi, hp, qi: (0, 0)),
            pl.BlockSpec((None, tq, w), lambda bi, hp, qi: (bi, qi, G_GA // w + hp)),
        ],
        out_specs=pl.BlockSpec((None, tq, w), lambda bi, hp, qi: (bi, qi, hp)),
        compiler_params=pltpu.CompilerParams(
            dimension_semantics=("arbitrary", "arbitrary", "arbitrary"),
            vmem_limit_bytes=VMEM_LIMIT_BYTES),
        name="sb_attention",
    )(p3, p3, vt, u2, g3)


def _sortable_key(x):
    bits = lax.bitcast_convert_type(x, jnp.int32)
    return bits ^ ((bits >> 31) & 0x7FFFFFFF)


def _dsa_kernel(rb_ref, far_ref, qi_ref, wi_ref, ki_ref, q_ref, k_ref, vt_ref, bkt_ref, g_ref,
                o_ref, key_ref, hi_ref, lo_ref, bias_ref, m_ref, acc_ref, *, tq, topk, seq):
    qi = pl.program_id(1)
    nblk = qi + 1

    @pl.when(jnp.logical_and(pl.program_id(0) == 0, qi == 0))
    def _():
        far = far_ref[0]
        for h in range(HEADS_B):
            for d in range(2):
                bkt = bkt_ref[d]

                def bucket_body(b, tile, bkt=bkt, h=h):
                    return jnp.where(bkt == b, (rb_ref[b, h] - rb_ref[far, h]) * LOG2E, tile)

                bias_ref[h, d] = lax.fori_loop(0, N_BUCKETS, bucket_body, jnp.zeros((tq, tq), F32))

    wi = wi_ref[...]
    qih = [_mask_head(qi_ref[:, (h // 2) * LANES:(h // 2 + 1) * LANES], h % 2)
           for h in range(IDX_HEADS)]

    def score_block(j, nb=1):
        start = pl.multiple_of(j * tq, tq)
        kib = ki_ref[pl.ds(start, tq * nb), :]
        rels = [_dot_nt(kib, qih[h]) for h in range(IDX_HEADS)]
        sc = jnp.zeros((tq * nb, tq), F32)
        for h in range(IDX_HEADS):
            sc = sc + jnp.maximum(rels[h], 0.0) * wi[h:h + 1, :]
        return sc

    def store_keys(j, sc):
        key = _sortable_key(sc)
        key_ref[j] = key
        hi_ref[j] = (key >> 16).astype(jnp.int16)

    def p1_body(i, c):
        sc = score_block(2 * i, 2)
        store_keys(2 * i, sc[:tq])
        store_keys(2 * i + 1, sc[tq:])
        return c

    lax.fori_loop(0, qi // 2, p1_body, 0)

    @pl.when(qi % 2 == 1)
    def _():
        store_keys(qi - 1, score_block(qi - 1))

    key_i = lax.broadcasted_iota(jnp.int32, (tq, tq), 0)
    qry_i = lax.broadcasted_iota(jnp.int32, (tq, tq), 1)
    admissible = key_i < (qry_i // CHUNK + 1) * CHUNK
    store_keys(qi, jnp.where(admissible, score_block(qi), -jnp.inf))

    def count(pred):
        def body(j, acc):
            ones = jnp.where(pred(key_ref[j], j * tq), 1.0, 0.0)
            for r in range(0, tq, SUBLANES):
                acc = acc + ones[r:r + SUBLANES, :]
            return acc
        acc = lax.fori_loop(0, nblk, body, jnp.zeros((SUBLANES, tq), F32))
        return jnp.sum(acc, axis=0, keepdims=True)

    def count16(ref, pred):
        rows = 2 * SUBLANES
        def body(j, acc):
            ones = jnp.where(pred(ref[j]), jnp.int16(1), jnp.int16(0))
            for r in range(0, tq, rows):
                acc = acc + ones[r:r + rows, :]
            return acc
        acc = lax.fori_loop(0, nblk, body, jnp.zeros((rows, tq), jnp.int16))
        return jnp.sum(acc.astype(F32), axis=0, keepdims=True)

    scanned = (nblk * tq).astype(F32)

    def search16(ref, target):
        def bit_body(i, carry):
            prefix, cnt_ge, cnt_gt = carry
            cand_u = prefix | jnp.left_shift(jnp.int32(1), 15 - i)
            cand = (cand_u - HALF16).astype(jnp.int16)
            cnt = count16(ref, lambda blk: blk >= cand)
            take = cnt >= target
            return (jnp.where(take, cand_u, prefix), jnp.where(take, cnt, cnt_ge),
                    jnp.where(take, cnt_gt, cnt))
        init = (jnp.zeros((1, tq), jnp.int32), jnp.ones((1, tq), F32) * scanned, jnp.zeros((1, tq), F32))
        prefix, cnt_ge, cnt_gt = lax.fori_loop(0, 16, bit_body, init)
        return prefix - HALF16, cnt_ge, cnt_gt

    kf = float(topk)
    assert seq // (2 * SUBLANES) < 2 ** 15
    thr_hi, hi_ge, hi_gt = search16(hi_ref, kf)
    thr_hi16 = thr_hi.astype(jnp.int16)

    def low_body(j, c):
        lo = ((key_ref[j] & 0xFFFF) - HALF16).astype(jnp.int16)
        lo_ref[j] = jnp.where(hi_ref[j] == thr_hi16, lo, jnp.int16(-HALF16))
        return c

    lax.fori_loop(0, nblk, low_body, 0)
    thr_lo, lo_ge, _ = search16(lo_ref, kf - hi_gt)
    thr = thr_hi * (2 * HALF16) + (thr_lo + HALF16)
    cnt_ge = hi_gt + jnp.where(thr_lo > -HALF16, lo_ge, hi_ge - hi_gt)

    tie = jnp.logical_and(cnt_ge > kf, thr > NEG_INF_KEY)
    has_tie = jnp.max(jnp.where(tie, 1.0, 0.0)) > 0.0

    def tie_cut():
        cnt_gt = count(lambda blk, k0: blk > thr)
        need = kf - cnt_gt
        nbits = int(seq).bit_length()

        def idx_body(i, lo):
            c_try = lo + jnp.left_shift(jnp.int32(1), nbits - 1 - i)
            f = count(lambda blk, k0: jnp.logical_and(blk == thr, (key_i + k0) < c_try))
            return jnp.where(f < need, c_try, lo)

        lo = lax.fori_loop(0, nbits, idx_body, jnp.zeros((1, tq), jnp.int32))
        return jnp.where(tie, lo + 1, jnp.int32(NO_CUT))

    cut = lax.cond(has_tie, tie_cut, lambda: jnp.full((1, tq), NO_CUT, jnp.int32))

    def mask_body(j, c):
        blk = key_ref[j]
        tie_sel = jnp.logical_and(blk == thr, (key_i + j * tq) < cut)
        sel = jnp.logical_and(jnp.logical_or(blk > thr, tie_sel), blk > NEG_INF_KEY)
        key_ref[j] = lax.bitcast_convert_type(jnp.where(sel, 0.0, NEG_MASK), jnp.int32)
        return c

    lax.fori_loop(0, nblk, mask_body, 0)

    qh = [_mask_head(q_ref[:, (h // 2) * LANES:(h // 2 + 1) * LANES], h % 2)
          for h in range(HEADS_B)]
    m_ref[...] = jnp.full(m_ref.shape, NEG_MASK, F32)
    acc_ref[...] = jnp.zeros(acc_ref.shape, F32)
    vrow = lax.broadcasted_iota(jnp.int32, (LANES, tq), 0)

    def att_block(j, nb, d, stream=False):
        start = pl.multiple_of(j * tq, tq)
        cat = lambda xs, axis: xs[0] if nb == 1 else jnp.concatenate(xs, axis=axis)
        mask = lax.bitcast_convert_type(cat([key_ref[j + i] for i in range(nb)], 0), F32)
        vts = []
        for hp in range(HEADS_B // HEADS_PER_TILE):
            vt = cat([vt_ref[j + i, hp * LANES:(hp + 1) * LANES, :] for i in range(nb)], 1)
            one = jnp.ones_like(vt)
            head0_rows = lax.broadcasted_iota(jnp.int32, vt.shape, 0) < HEAD_DIM
            vts.append([jnp.where(head0_rows, vt, one), jnp.where(head0_rows, one, vt)])
        pair = lambda h: slice((h // HEADS_PER_TILE) * LANES, (h // HEADS_PER_TILE + 1) * LANES)
        qk = lambda h: _dot_nt(k_ref[pl.ds(start, tq * nb), pair(h)], qh[h])
        m_old = m_ref[...]
        hs = range(HEADS_B)
        if stream:
            logits = [qk(h) + mask for h in hs]
            ps = [jnp.exp2(logits[h] - m_old[h:h + 1, :]).astype(BF16) for h in hs]
            bmax = jnp.concatenate([jnp.max(logits[h], axis=0, keepdims=True) for h in hs], axis=0)
            pvs = [_dot(vts[h // HEADS_PER_TILE][h % HEADS_PER_TILE], ps[h]) for h in hs]
            ok = jnp.max(bmax - m_old) <= SOFTMAX_SLACK

            @pl.when(ok)
            def _():
                for h in hs:
                    acc_ref[h] = acc_ref[h] + pvs[h]

            return ok
        logits = [qk(h) + mask for h in hs]
        if d is not None:
            logits = [lg + bias_ref[h, d] for lg, h in zip(logits, hs)]
        m_new = [jnp.maximum(m_old[h:h + 1, :], jnp.max(logits[h], axis=0, keepdims=True)) for h in hs]
        ps = [jnp.exp2(logits[h] - m_new[h]).astype(BF16) for h in hs]
        pvs = [_dot(vts[h // HEADS_PER_TILE][h % HEADS_PER_TILE], ps[h]) for h in hs]
        accs = [jnp.exp2(m_old[h:h + 1, :] - m_new[h]) * acc_ref[h] + pvs[h] for h in hs]
        m_ref[...] = jnp.concatenate(m_new, axis=0)
        for h in range(HEADS_B):
            acc_ref[h] = accs[h]

    att_block(qi, 1, 0)

    @pl.when(qi > 0)
    def _():
        att_block(qi - 1, 1, 1)

    n_far = qi - 1

    def far_blocks(j, nb):
        ok = att_block(j, nb, None, stream=True)

        @pl.when(jnp.logical_not(ok))
        def _():
            att_block(j, nb, None)

    def far_body(i, c):
        far_blocks(2 * i, 2)
        return c

    lax.fori_loop(0, n_far // 2, far_body, 0)

    @pl.when(jnp.logical_and(n_far > 0, n_far % 2 == 1))
    def _():
        far_blocks(n_far - 1, 1)

    for hp in range(HEADS_B // HEADS_PER_TILE):
        h0 = hp * HEADS_PER_TILE
        a0, a1 = acc_ref[h0], acc_ref[h0 + 1]
        o0 = a0 / a0[LANES - 1:LANES, :]
        o1 = a1 / a1[0:1, :]
        ot = jnp.where(vrow < HEAD_DIM, o0, o1)
        gp = g_ref[:, hp * LANES:(hp + 1) * LANES]
        o_ref[:, hp * LANES:(hp + 1) * LANES] = (ot.T * _silu(gp)).astype(o_ref.dtype)


def _t5_bucket(rel):
    half = N_BUCKETS // 2
    max_exact = half // 2
    ret = jnp.where(rel > 0, half, 0)
    n = jnp.abs(rel)
    nf = jnp.maximum(n, 1).astype(jnp.float32)
    large = max_exact + (jnp.log(nf / max_exact) / math.log(MAX_DISTANCE / max_exact)
                         * (half - max_exact)).astype(jnp.int32)
    large = jnp.minimum(large, half - 1)
    return ret + jnp.where(n < max_exact, n, large)


def _dsa_attention(p3, g3, vt, wi_t, rel_bias, tq, topk):
    b, s, _ = p3.shape
    nkb = s // tq
    assert tq >= MAX_DISTANCE
    key = jnp.arange(tq, dtype=jnp.int32)[:, None]
    qry = jnp.arange(tq, dtype=jnp.int32)[None, :]
    buckets = jnp.stack([_t5_bucket(key - qry), _t5_bucket(key - qry - tq)])
    far = _t5_bucket(jnp.full((1,), -tq - 1, jnp.int32))
    row = lambda blk: (lambda bi, qi: (bi, qi, blk))
    full = lambda blk: (lambda bi, qi: (bi, 0, blk))
    smem = pl.BlockSpec(memory_space=pltpu.SMEM)
    return pl.pallas_call(
        functools.partial(_dsa_kernel, tq=tq, topk=topk, seq=s),
        out_shape=jax.ShapeDtypeStruct((b, s, WIDTH_B), BF16),
        grid=(b, nkb),
        in_specs=[
            smem, smem,
            pl.BlockSpec((None, tq, 512), row(P_QI // 512)),
            pl.BlockSpec((T_WI_ROWS, tq), lambda bi, qi: (0, bi * nkb + qi)),
            pl.BlockSpec((None, s, LANES), full(P_KI // LANES)),
            pl.BlockSpec((None, tq, 512), row(P_QB // 512)),
            pl.BlockSpec((None, s, 512), full(P_KB // 512)),
            pl.BlockSpec((nkb, WIDTH_B, tq), lambda bi, qi: (bi, T_VB // WIDTH_B, 0)),
            pl.BlockSpec((2, tq, tq), lambda bi, qi: (0, 0, 0)),
            pl.BlockSpec((None, tq, 512), row(G_GB // 512)),
        ],
        out_specs=pl.BlockSpec((None, tq, WIDTH_B), lambda bi, qi: (bi, qi, 0)),
        scratch_shapes=[pltpu.VMEM((nkb, tq, tq), jnp.int32),
                        pltpu.VMEM((nkb, tq, tq), jnp.int16),
                        pltpu.VMEM((nkb, tq, tq), jnp.int16),
                        pltpu.VMEM((HEADS_B, 2, tq, tq), F32),
                        pltpu.VMEM((HEADS_B, tq), F32),
                        pltpu.VMEM((HEADS_B, LANES, tq), F32)],
        compiler_params=pltpu.CompilerParams(
            dimension_semantics=("arbitrary", "arbitrary"),
            vmem_limit_bytes=VMEM_LIMIT_BYTES),
        name="dsa_attention",
    )(rel_bias, far, p3, wi_t, p3, p3, p3, vt, buckets, g3)


def _out_kernel(ma_ref, mb_ref, x_ref, w_ref, lg_ref, lb_ref, o_ref, *, alpha):
    y = _dot(ma_ref[...], w_ref[:WIDTH_A, :]) + _dot(mb_ref[...], w_ref[WIDTH_A:, :])
    h = alpha * x_ref[...] + y
    mu = jnp.mean(h, axis=-1, keepdims=True)
    hc = h - mu
    var = jnp.mean(hc * hc, axis=-1, keepdims=True)
    o_ref[...] = hc * lax.rsqrt(var + LN_EPS) * lg_ref[...] + lb_ref[...]


def _out_proj_ln(mix_a, mix_b, x2, w_out, ln_g, ln_b, alpha, tm):
    m, d = x2.shape
    return pl.pallas_call(
        functools.partial(_out_kernel, alpha=alpha),
        out_shape=jax.ShapeDtypeStruct((m, d), F32),
        grid=(m // tm,),
        in_specs=[pl.BlockSpec((tm, WIDTH_A), lambda i: (i, 0)),
                  pl.BlockSpec((tm, WIDTH_B), lambda i: (i, 0)),
                  pl.BlockSpec((tm, d), lambda i: (i, 0)),
                  pl.BlockSpec((WIDTH_A + WIDTH_B, d), lambda i: (0, 0)),
                  pl.BlockSpec((1, d), lambda i: (0, 0)),
                  pl.BlockSpec((1, d), lambda i: (0, 0))],
        out_specs=pl.BlockSpec((tm, d), lambda i: (i, 0)),
        compiler_params=pltpu.CompilerParams(
            dimension_semantics=("arbitrary",), vmem_limit_bytes=VMEM_LIMIT_BYTES),
        name="out_proj_ln",
    )(mix_a, mix_b, x2, w_out, ln_g, ln_b)


def _pack_w_in(w):
    scale = LOG2E / math.sqrt(HEAD_DIM)
    sizes = (WIDTH_A,) * 4 + (WIDTH_B,) * 4 + (IDX_HEADS * IDX_DIM, IDX_DIM, IDX_HEADS)
    qa, ka, va, ga, qb, kb, vb, gb, qi, ki, wi = jnp.split(w, np.cumsum(sizes)[:-1].tolist(), axis=1)
    w_row = jnp.concatenate([qa * scale, ka, qb * scale, kb, qi, ki, ki, ga, gb], axis=1)
    wi_pad = jnp.pad(wi, ((0, 0), (0, T_WI_ROWS - IDX_HEADS)))
    w_t = jnp.concatenate([va, vb, wi_pad], axis=1).T
    return w_row.astype(BF16), w_t.astype(BF16)


def _layer(x, w_in, w_out, ln_g, ln_b, rel_bias, alpha, topk):
    b, s, d = x.shape
    tq = min(256, s)
    tm = min(512, b * s)
    x2 = x.reshape(b * s, d)
    w_row, w_t = _pack_w_in(w_in)
    p2, g2, vt, wi_t = _proj(x2, w_row, w_t, tm=tm, tk=tq)
    p3 = p2.reshape(b, s, P_COLS)
    g3 = g2.reshape(b, s, G_COLS)
    mix_a = _sb_attention(p3, g3, vt, tq)
    mix_b = _dsa_attention(p3, g3, vt, wi_t, rel_bias, tq, topk)
    out = _out_proj_ln(mix_a.reshape(b * s, WIDTH_A), mix_b.reshape(b * s, WIDTH_B), x2,
                       w_out.astype(BF16), ln_g.reshape(1, d), ln_b.reshape(1, d), alpha, tm=tm)
    return out.reshape(b, s, d)


def kernel(x, w_in, w_out, ln_g, ln_b, rel_bias):
    depth = w_in.shape[0]
    alpha = (2.0 * depth) ** 0.25
    topk = min(TOPK_MAX, x.shape[1] // 4)
    h = x
    for layer in range(depth):
        h = _layer(h, w_in[layer], w_out[layer], ln_g[layer], ln_b[layer], rel_bias, alpha, topk)
    return h
```

```python
import functools
import math

import jax
import jax.numpy as jnp
import numpy as np
from jax import lax
from jax.experimental import pallas as pl
from jax.experimental.pallas import tpu as pltpu

HEAD_DIM = 64
HEADS_A = 8
HEADS_B = 8
WIDTH_A = HEADS_A * HEAD_DIM
WIDTH_B = HEADS_B * HEAD_DIM
IDX_HEADS = 8
IDX_DIM = 64
CHUNK = 64
TOPK_MAX = 256
N_BUCKETS = 32
MAX_DISTANCE = 128
LN_EPS = 1e-5

LANES = 128
SUBLANES = 8
HEADS_PER_TILE = LANES // HEAD_DIM
VMEM_LIMIT_BYTES = 56 * 1024 * 1024

P_QA, P_KA, P_QB, P_KB, P_QI = (i * 512 for i in range(5))
P_KI = 5 * 512
P_COLS = P_KI + LANES
G_GA, G_GB = 0, 512
G_COLS = 1024
T_VA, T_VB, T_WI = 0, 512, 1024
T_WI_ROWS = 16
T_ROWS = T_WI + T_WI_ROWS

NEG_MASK = -1e30
INT_MIN = -2 ** 31
NEG_INF_KEY = -2139095041
NO_CUT = 2 ** 30
HALF16 = 2 ** 15
SOFTMAX_SLACK = 40.0
LOG2E = math.log2(math.e)
DENOM_ROWS = 16
FAR_GROUP = 4
SB_HEADS_PER_STEP = 8
LOG2_DEAD = -151.0

BF16 = jnp.bfloat16
F32 = jnp.float32


def _dot_nt(a, b):
    return lax.dot_general(a, b, (((1,), (1,)), ((), ())), preferred_element_type=F32)


def _dot(a, b):
    return jnp.dot(a, b, preferred_element_type=F32)


def _mask_head(x, parity):
    lane = lax.broadcasted_iota(jnp.int32, x.shape, 1)
    keep = (lane < HEAD_DIM) if parity == 0 else (lane >= HEAD_DIM)
    return jnp.where(keep, x, jnp.zeros_like(x))


def _tree_sum(xs):
    while len(xs) > 1:
        xs = [a + b for a, b in zip(xs[::2], xs[1::2])] + ([xs[-1]] if len(xs) % 2 else [])
    return xs[0]


def _silu(g):
    return g / (1.0 + jnp.exp(-g))


def _proj_kernel(x_ref, wr_ref, wt_ref, p_ref, g_ref, vt_ref, wi_ref, *, tk):
    xb = x_ref[...].astype(BF16)
    for c in range(0, P_COLS, 512):
        cw = min(512, P_COLS - c)
        p_ref[:, c:c + cw] = _dot(xb, wr_ref[:, c:c + cw]).astype(BF16)
    for c in range(0, G_COLS, 512):
        g_ref[:, c:c + 512] = _dot(xb, wr_ref[:, P_COLS + c:P_COLS + c + 512])
    for r in range(0, T_WI, 512):
        t = _dot_nt(wt_ref[r:r + 512, :], xb).astype(BF16)
        for kb in range(xb.shape[0] // tk):
            vt_ref[kb, r:r + 512, :] = t[:, kb * tk:(kb + 1) * tk]
    wi_ref[...] = _dot_nt(wt_ref[T_WI:, :], xb)


def _proj(x2, w_row, w_t, tm, tk):
    m, d = x2.shape
    return pl.pallas_call(
        functools.partial(_proj_kernel, tk=tk),
        out_shape=(jax.ShapeDtypeStruct((m, P_COLS), BF16),
                   jax.ShapeDtypeStruct((m, G_COLS), F32),
                   jax.ShapeDtypeStruct((m // tk, T_WI, tk), BF16),
                   jax.ShapeDtypeStruct((T_WI_ROWS, m), F32)),
        grid=(m // tm,),
        in_specs=[pl.BlockSpec((tm, d), lambda i: (i, 0)),
                  pl.BlockSpec((d, P_COLS + G_COLS), lambda i: (0, 0)),
                  pl.BlockSpec((T_ROWS, d), lambda i: (0, 0))],
        out_specs=(pl.BlockSpec((tm, P_COLS), lambda i: (i, 0)),
                   pl.BlockSpec((tm, G_COLS), lambda i: (i, 0)),
                   pl.BlockSpec((tm // tk, T_WI, tk), lambda i: (i, 0, 0)),
                   pl.BlockSpec((T_WI_ROWS, tm), lambda i: (0, i))),
        compiler_params=pltpu.CompilerParams(
            dimension_semantics=("arbitrary",), vmem_limit_bytes=VMEM_LIMIT_BYTES),
        name="proj",
    )(x2, w_row, w_t)


def _softplus2(z):
    return jnp.maximum(z, 0.0) + jnp.log2(1.0 + jnp.exp2(-jnp.abs(z)))


def _sb_blocks(chains, u2):
    half = u2.shape[0]
    stricts = [c[3] for c in chains]
    masked = lambda xs: [x if m is None else jnp.where(m, x, 0.0) for x, m in zip(xs, stricts)]
    zts = [_dot_nt(kblk, qh) for qh, kblk, _, _ in chains]
    sps = masked([_softplus2(zt) for zt in zts])
    his = [sp.astype(BF16) for sp in sps]
    los = [(sp - hi.astype(F32)).astype(BF16) for sp, hi in zip(sps, his)]
    halves = lambda x: (x[:half], x[half:])
    tails = []
    for sp, hi, lo in zip(sps, his, los):
        (hi_a, hi_b), (lo_a, lo_b) = halves(hi), halves(lo)
        tail_a = _dot(u2, jnp.concatenate([hi_a, lo_a], axis=0))
        tail_b = _dot(u2, jnp.concatenate([hi_b, lo_b], axis=0))
        total_b = tail_b[:1, :] - sp[half:half + 1, :]
        tails.append(jnp.concatenate([tail_a + total_b, tail_b], axis=0))
    ws = masked([jnp.exp2(zt - sp + tail) for zt, sp, tail in zip(zts, sps, tails)])
    pvs = [_dot(c[2], a.astype(BF16)) for c, a in zip(chains, ws)]
    totals = [tail[:1, :] - sp[:1, :] for tail, sp in zip(tails, sps)]
    return list(zip(pvs, totals))


def _sb_kernel(q_ref, k_ref, vt_ref, u_ref, g_ref, o_ref, *, tq, n_heads):
    qi = pl.program_id(2)
    pair = lambda h: slice((h // HEADS_PER_TILE) * LANES, (h // HEADS_PER_TILE + 1) * LANES)
    qh = [_mask_head(q_ref[:, pair(h)], h % HEADS_PER_TILE) for h in range(n_heads)]
    u2 = u_ref[...]
    key_i = lax.broadcasted_iota(jnp.int32, (tq, tq), 0)
    qry_i = lax.broadcasted_iota(jnp.int32, (tq, tq), 1)
    strict = key_i < qry_i

    def chains(j, mask):
        start = pl.multiple_of(j * tq, tq)
        return [(qh[h], k_ref[pl.ds(start, tq), pair(h)], vt_ref[j, pair(h), :], mask)
                for h in range(n_heads)]

    has_prev = qi > 0
    res = _sb_blocks(chains(qi, strict) + chains(jnp.maximum(qi - 1, 0), None), u2)
    state = []
    for h in range(n_heads):
        (pv_d, tot_d), (pv_p, tot_p) = res[h], res[n_heads + h]
        state += [pv_d + jnp.where(has_prev, jnp.exp2(tot_d), 0.0) * pv_p, tot_d + tot_p]

    def alive_after(st):
        top = functools.reduce(jnp.maximum, st[1::2])
        return jnp.max(top) > LOG2_DEAD

    def cond(carry):
        j, alive = carry[0], carry[1]
        return jnp.logical_and(j >= 0, alive)

    def body(carry):
        j, st = carry[0], carry[2:]
        res = _sb_blocks(chains(j, None), u2)
        out = []
        for h in range(n_heads):
            acc, run = st[2 * h], st[2 * h + 1]
            pv, total = res[h]
            out += [acc + jnp.exp2(run) * pv, run + total]
        return (j - 1, alive_after(out)) + tuple(out)

    fin = lax.while_loop(cond, body, (qi - 2, alive_after(state)) + tuple(state))[2:]
    row = lax.broadcasted_iota(jnp.int32, fin[0].shape, 0)
    for hp in range(n_heads // HEADS_PER_TILE):
        acc0, acc1 = fin[2 * (2 * hp)], fin[2 * (2 * hp + 1)]
        ot = jnp.where(row < HEAD_DIM, acc0, acc1)
        o_ref[:, pair(2 * hp)] = (ot.T * _silu(g_ref[:, pair(2 * hp)])).astype(o_ref.dtype)


def _sb_attention(p3, g3, vt, tq):
    b, s, _ = p3.shape
    nkb = s // tq
    half = tq // 2
    tri = np.triu(-np.ones((half, half), np.float32), 1)
    u2 = jnp.asarray(np.concatenate([tri, tri], axis=1), BF16)
    w = SB_HEADS_PER_STEP * HEAD_DIM
    return pl.pallas_call(
        functools.partial(_sb_kernel, tq=tq, n_heads=SB_HEADS_PER_STEP),
        out_shape=jax.ShapeDtypeStruct((b, s, WIDTH_A), BF16),
        grid=(b, HEADS_A // SB_HEADS_PER_STEP, nkb),
        in_specs=[
            pl.BlockSpec((None, tq, w), lambda bi, hp, qi: (bi, qi, P_QA // w + hp)),
            pl.BlockSpec((None, s, w), lambda bi, hp, qi: (bi, 0, P_KA // w + hp)),
            pl.BlockSpec((nkb, w, tq), lambda bi, hp, qi: (bi, T_VA // w + hp, 0)),
            pl.BlockSpec((half, 2 * half), lambda bi, hp, qi: (0, 0)),
            pl.BlockSpec((None, tq, w), lambda bi, hp, qi: (bi, qi, G_GA // w + hp)),
        ],
        out_specs=pl.BlockSpec((None, tq, w), lambda bi, hp, qi: (bi, qi, hp)),
        compiler_params=pltpu.CompilerParams(
            dimension_semantics=("arbitrary", "arbitrary", "arbitrary"),
            vmem_limit_bytes=VMEM_LIMIT_BYTES),
        name="sb_attention",
    )(p3, p3, vt, u2, g3)


def _sortable_key(x):
    bits = lax.bitcast_convert_type(x, jnp.int32)
    return bits ^ ((bits >> 31) & 0x7FFFFFFF)


def _dsa_kernel(rb_ref, far_ref, qi_ref, wi_ref, ki_ref, q_ref, k_ref, vt_ref, bkt_ref, g_ref,
                o_ref, key_ref, hi_ref, lo_ref, bias_ref, m_ref, acc_ref, *, tq, topk, seq):
    qi = pl.program_id(1)
    nblk = qi + 1

    @pl.when(jnp.logical_and(pl.program_id(0) == 0, qi == 0))
    def _():
        far = far_ref[0]
        for h in range(HEADS_B):
            for d in range(2):
                bkt = bkt_ref[d]

                def bucket_body(b, tile, bkt=bkt, h=h):
                    return jnp.where(bkt == b, (rb_ref[b, h] - rb_ref[far, h]) * LOG2E, tile)

                bias_ref[h, d] = lax.fori_loop(0, N_BUCKETS, bucket_body, jnp.zeros((tq, tq), F32))

    wi = wi_ref[...]
    qih = [_mask_head(qi_ref[:, (h // 2) * LANES:(h // 2 + 1) * LANES], h % 2)
           for h in range(IDX_HEADS)]

    def score_block(j, nb=1):
        start = pl.multiple_of(j * tq, tq)
        kib = ki_ref[pl.ds(start, tq * nb), :]
        rels = [_dot_nt(kib, qih[h]) for h in range(IDX_HEADS)]
        sc = jnp.zeros((tq * nb, tq), F32)
        for h in range(IDX_HEADS):
            sc = sc + jnp.maximum(rels[h], 0.0) * wi[h:h + 1, :]
        return sc

    def store_keys(j, sc):
        key = _sortable_key(sc)
        key_ref[j] = key
        hi_ref[j] = (key >> 16).astype(jnp.int16)

    def p1_body(i, c):
        sc = score_block(2 * i, 2)
        store_keys(2 * i, sc[:tq])
        store_keys(2 * i + 1, sc[tq:])
        return c

    lax.fori_loop(0, qi // 2, p1_body, 0)

    @pl.when(qi % 2 == 1)
    def _():
        store_keys(qi - 1, score_block(qi - 1))

    key_i = lax.broadcasted_iota(jnp.int32, (tq, tq), 0)
    qry_i = lax.broadcasted_iota(jnp.int32, (tq, tq), 1)
    admissible = key_i < (qry_i // CHUNK + 1) * CHUNK
    store_keys(qi, jnp.where(admissible, score_block(qi), -jnp.inf))

    def count(pred):
        def body(j, acc):
            ones = jnp.where(pred(key_ref[j], j * tq), 1.0, 0.0)
            return acc + _tree_sum([ones[r:r + SUBLANES, :] for r in range(0, tq, SUBLANES)])
        acc = lax.fori_loop(0, nblk, body, jnp.zeros((SUBLANES, tq), F32))
        return jnp.sum(acc, axis=0, keepdims=True)

    def count16(ref, pred):
        rows = 2 * SUBLANES
        def body(i, acc):
            for blk in (ref[2 * i], ref[2 * i + 1]):
                ones = jnp.where(pred(blk), jnp.int16(1), jnp.int16(0))
                acc = acc + _tree_sum([ones[r:r + rows, :] for r in range(0, tq, rows)])
            return acc
        acc = lax.fori_loop(0, (nblk + 1) // 2, body, jnp.zeros((rows, tq), jnp.int16))
        return jnp.sum(acc.astype(F32), axis=0, keepdims=True)

    def pad16(ref):
        @pl.when(nblk % 2 == 1)
        def _():
            ref[nblk] = jnp.full((tq, tq), -HALF16, jnp.int16)

    scanned = (nblk * tq).astype(F32)

    def search16(ref, target):
        def bit_body(i, carry):
            prefix, cnt_ge, cnt_gt = carry
            cand_u = prefix | jnp.left_shift(jnp.int32(1), 15 - i)
            cand = (cand_u - HALF16).astype(jnp.int16)
            cnt = count16(ref, lambda blk: blk >= cand)
            take = cnt >= target
            return (jnp.where(take, cand_u, prefix), jnp.where(take, cnt, cnt_ge),
                    jnp.where(take, cnt_gt, cnt))
        init = (jnp.zeros((1, tq), jnp.int32), jnp.ones((1, tq), F32) * scanned, jnp.zeros((1, tq), F32))
        prefix, cnt_ge, cnt_gt = lax.fori_loop(0, 16, bit_body, init)
        return prefix - HALF16, cnt_ge, cnt_gt

    kf = float(topk)
    assert seq // (2 * SUBLANES) < 2 ** 15
    pad16(hi_ref)
    thr_hi, hi_ge, hi_gt = search16(hi_ref, kf)
    thr_hi16 = thr_hi.astype(jnp.int16)

    def low_body(j, c):
        lo = ((key_ref[j] & 0xFFFF) - HALF16).astype(jnp.int16)
        lo_ref[j] = jnp.where(hi_ref[j] == thr_hi16, lo, jnp.int16(-HALF16))
        return c

    lax.fori_loop(0, nblk, low_body, 0)
    pad16(lo_ref)
    thr_lo, lo_ge, _ = search16(lo_ref, kf - hi_gt)
    thr = thr_hi * (2 * HALF16) + (thr_lo + HALF16)
    cnt_ge = hi_gt + jnp.where(thr_lo > -HALF16, lo_ge, hi_ge - hi_gt)

    finite_thr = thr > NEG_INF_KEY
    tie = jnp.logical_and(cnt_ge > kf, finite_thr)
    has_tie = jnp.max(jnp.where(tie, 1.0, 0.0)) > 0.0

    def store_mask(j, additive):
        key_ref[j] = lax.bitcast_convert_type(additive, jnp.int32)

    @pl.when(jnp.logical_not(has_tie))
    def _():
        thr_min = jnp.where(finite_thr, thr, NEG_INF_KEY + 1)

        def mask_body(j, c):
            store_mask(j, jnp.where(key_ref[j] >= thr_min, 0.0, NEG_MASK))
            return c

        lax.fori_loop(0, nblk, mask_body, 0)

    @pl.when(has_tie)
    def _():
        cnt_gt = count(lambda blk, k0: blk > thr)
        need = kf - cnt_gt
        nbits = int(seq).bit_length()

        def idx_body(i, lo):
            c_try = lo + jnp.left_shift(jnp.int32(1), nbits - 1 - i)
            f = count(lambda blk, k0: jnp.logical_and(blk == thr, (key_i + k0) < c_try))
            return jnp.where(f < need, c_try, lo)

        lo = lax.fori_loop(0, nbits, idx_body, jnp.zeros((1, tq), jnp.int32))
        cut = jnp.where(tie, lo + 1, jnp.where(finite_thr, NO_CUT, 0))

        def mask_body(j, c):
            blk = key_ref[j]
            at_thr = jnp.where((key_i + j * tq) < cut, 0.0, NEG_MASK)
            store_mask(j, jnp.where(blk > thr, 0.0, jnp.where(blk == thr, at_thr, NEG_MASK)))
            return c

        lax.fori_loop(0, nblk, mask_body, 0)

    qh = [_mask_head(q_ref[:, (h // 2) * LANES:(h // 2 + 1) * LANES], h % 2)
          for h in range(HEADS_B)]
    m_ref[...] = jnp.full(m_ref.shape, NEG_MASK, F32)
    acc_ref[...] = jnp.zeros(acc_ref.shape, F32)

    def att_block(j, nb, d, stream=False):
        start = pl.multiple_of(j * tq, tq)
        cat = lambda xs, axis: xs[0] if nb == 1 else jnp.concatenate(xs, axis=axis)
        mask = lax.bitcast_convert_type(cat([key_ref[j + i] for i in range(nb)], 0), F32)
        ones = jnp.ones((DENOM_ROWS, tq * nb), BF16)
        vts = [jnp.concatenate(
            [cat([vt_ref[j + i, h * HEAD_DIM:(h + 1) * HEAD_DIM, :] for i in range(nb)], 1), ones], axis=0)
            for h in range(HEADS_B)]
        pair = lambda h: slice((h // HEADS_PER_TILE) * LANES, (h // HEADS_PER_TILE + 1) * LANES)
        qk = lambda h: _dot_nt(k_ref[pl.ds(start, tq * nb), pair(h)], qh[h])
        m_old = m_ref[...]
        hs = range(HEADS_B)
        if stream:
            logits = [qk(h) + mask for h in hs]
            ps = [jnp.exp2(logits[h] - m_old[h:h + 1, :]).astype(BF16) for h in hs]
            bmax = jnp.concatenate([jnp.max(logits[h], axis=0, keepdims=True) for h in hs], axis=0)
            pvs = [_dot(vts[h], ps[h]) for h in hs]
            ok = jnp.max(bmax - m_old) <= SOFTMAX_SLACK

            @pl.when(ok)
            def _():
                for h in hs:
                    acc_ref[h] = acc_ref[h] + pvs[h]

            return ok
        logits = [qk(h) + mask for h in hs]
        if d is not None:
            logits = [lg + bias_ref[h, d] for lg, h in zip(logits, hs)]
        m_new = [jnp.maximum(m_old[h:h + 1, :], jnp.max(logits[h], axis=0, keepdims=True)) for h in hs]
        ps = [jnp.exp2(logits[h] - m_new[h]).astype(BF16) for h in hs]
        pvs = [_dot(vts[h], ps[h]) for h in hs]
        accs = [jnp.exp2(m_old[h:h + 1, :] - m_new[h]) * acc_ref[h] + pvs[h] for h in hs]
        m_ref[...] = jnp.concatenate(m_new, axis=0)
        for h in range(HEADS_B):
            acc_ref[h] = accs[h]

    att_block(qi, 1, 0)

    @pl.when(qi > 0)
    def _():
        att_block(qi - 1, 1, 1)

    n_far = qi - 1

    def far_blocks(j, nb):
        ok = att_block(j, nb, None, stream=True)

        @pl.when(jnp.logical_not(ok))
        def _():
            att_block(j, nb, None)

    def far_body(i, c):
        far_blocks(FAR_GROUP * i, FAR_GROUP)
        return c

    n_groups = jnp.maximum(n_far, 0) // FAR_GROUP
    lax.fori_loop(0, n_groups, far_body, 0)
    done = n_groups * FAR_GROUP
    size = FAR_GROUP // 2
    while size >= 1:
        take = jnp.logical_and(n_far > 0, (n_far // size) % 2 == 1)

        @pl.when(take)
        def _(done=done, size=size):
            far_blocks(done, size)

        done = done + jnp.where(take, size, 0)
        size //= 2

    for hp in range(HEADS_B // HEADS_PER_TILE):
        h0 = hp * HEADS_PER_TILE
        a0, a1 = acc_ref[h0], acc_ref[h0 + 1]
        ot = jnp.concatenate([a0[:HEAD_DIM] / a0[HEAD_DIM:HEAD_DIM + 1, :],
                              a1[:HEAD_DIM] / a1[HEAD_DIM:HEAD_DIM + 1, :]], axis=0)
        gp = g_ref[:, hp * LANES:(hp + 1) * LANES]
        o_ref[:, hp * LANES:(hp + 1) * LANES] = (ot.T * _silu(gp)).astype(o_ref.dtype)


def _t5_bucket(rel):
    half = N_BUCKETS // 2
    max_exact = half // 2
    ret = jnp.where(rel > 0, half, 0)
    n = jnp.abs(rel)
    nf = jnp.maximum(n, 1).astype(jnp.float32)
    large = max_exact + (jnp.log(nf / max_exact) / math.log(MAX_DISTANCE / max_exact)
                         * (half - max_exact)).astype(jnp.int32)
    large = jnp.minimum(large, half - 1)
    return ret + jnp.where(n < max_exact, n, large)


def _dsa_attention(p3, g3, vt, wi_t, rel_bias, tq, topk):
    b, s, _ = p3.shape
    nkb = s // tq
    assert tq >= MAX_DISTANCE
    key = jnp.arange(tq, dtype=jnp.int32)[:, None]
    qry = jnp.arange(tq, dtype=jnp.int32)[None, :]
    buckets = jnp.stack([_t5_bucket(key - qry), _t5_bucket(key - qry - tq)])
    far = _t5_bucket(jnp.full((1,), -tq - 1, jnp.int32))
    row = lambda blk: (lambda bi, qi: (bi, qi, blk))
    full = lambda blk: (lambda bi, qi: (bi, 0, blk))
    smem = pl.BlockSpec(memory_space=pltpu.SMEM)
    return pl.pallas_call(
        functools.partial(_dsa_kernel, tq=tq, topk=topk, seq=s),
        out_shape=jax.ShapeDtypeStruct((b, s, WIDTH_B), BF16),
        grid=(b, nkb),
        in_specs=[
            smem, smem,
            pl.BlockSpec((None, tq, 512), row(P_QI // 512)),
            pl.BlockSpec((T_WI_ROWS, tq), lambda bi, qi: (0, bi * nkb + qi)),
            pl.BlockSpec((None, s, LANES), full(P_KI // LANES)),
            pl.BlockSpec((None, tq, 512), row(P_QB // 512)),
            pl.BlockSpec((None, s, 512), full(P_KB // 512)),
            pl.BlockSpec((nkb, WIDTH_B, tq), lambda bi, qi: (bi, T_VB // WIDTH_B, 0)),
            pl.BlockSpec((2, tq, tq), lambda bi, qi: (0, 0, 0)),
            pl.BlockSpec((None, tq, 512), row(G_GB // 512)),
        ],
        out_specs=pl.BlockSpec((None, tq, WIDTH_B), lambda bi, qi: (bi, qi, 0)),
        scratch_shapes=[pltpu.VMEM((nkb, tq, tq), jnp.int32),
                        pltpu.VMEM((nkb + nkb % 2, tq, tq), jnp.int16),
                        pltpu.VMEM((nkb + nkb % 2, tq, tq), jnp.int16),
                        pltpu.VMEM((HEADS_B, 2, tq, tq), F32),
                        pltpu.VMEM((HEADS_B, tq), F32),
                        pltpu.VMEM((HEADS_B, HEAD_DIM + DENOM_ROWS, tq), F32)],
        compiler_params=pltpu.CompilerParams(
            dimension_semantics=("arbitrary", "arbitrary"),
            vmem_limit_bytes=VMEM_LIMIT_BYTES),
        name="dsa_attention",
    )(rel_bias, far, p3, wi_t, p3, p3, p3, vt, buckets, g3)


def _out_kernel(ma_ref, mb_ref, x_ref, w_ref, lg_ref, lb_ref, o_ref, *, alpha):
    y = _dot(ma_ref[...], w_ref[:WIDTH_A, :]) + _dot(mb_ref[...], w_ref[WIDTH_A:, :])
    h = alpha * x_ref[...] + y
    mu = jnp.mean(h, axis=-1, keepdims=True)
    hc = h - mu
    var = jnp.mean(hc * hc, axis=-1, keepdims=True)
    o_ref[...] = hc * lax.rsqrt(var + LN_EPS) * lg_ref[...] + lb_ref[...]


def _out_proj_ln(mix_a, mix_b, x2, w_out, ln_g, ln_b, alpha, tm):
    m, d = x2.shape
    return pl.pallas_call(
        functools.partial(_out_kernel, alpha=alpha),
        out_shape=jax.ShapeDtypeStruct((m, d), F32),
        grid=(m // tm,),
        in_specs=[pl.BlockSpec((tm, WIDTH_A), lambda i: (i, 0)),
                  pl.BlockSpec((tm, WIDTH_B), lambda i: (i, 0)),
                  pl.BlockSpec((tm, d), lambda i: (i, 0)),
                  pl.BlockSpec((WIDTH_A + WIDTH_B, d), lambda i: (0, 0)),
                  pl.BlockSpec((1, d), lambda i: (0, 0)),
                  pl.BlockSpec((1, d), lambda i: (0, 0))],
        out_specs=pl.BlockSpec((tm, d), lambda i: (i, 0)),
        compiler_params=pltpu.CompilerParams(
            dimension_semantics=("arbitrary",), vmem_limit_bytes=VMEM_LIMIT_BYTES),
        name="out_proj_ln",
    )(mix_a, mix_b, x2, w_out, ln_g, ln_b)


def _pack_w_in(w):
    scale = LOG2E / math.sqrt(HEAD_DIM)
    sizes = (WIDTH_A,) * 4 + (WIDTH_B,) * 4 + (IDX_HEADS * IDX_DIM, IDX_DIM, IDX_HEADS)
    qa, ka, va, ga, qb, kb, vb, gb, qi, ki, wi = jnp.split(w, np.cumsum(sizes)[:-1].tolist(), axis=1)
    w_row = jnp.concatenate([qa * scale, ka, qb * scale, kb, qi, ki, ki, ga, gb], axis=1)
    wi_pad = jnp.pad(wi, ((0, 0), (0, T_WI_ROWS - IDX_HEADS)))
    w_t = jnp.concatenate([va, vb, wi_pad], axis=1).T
    return w_row.astype(BF16), w_t.astype(BF16)


def _layer(x, w_in, w_out, ln_g, ln_b, rel_bias, alpha, topk):
    b, s, d = x.shape
    tq = min(256, s)
    tm = min(512, b * s)
    x2 = x.reshape(b * s, d)
    w_row, w_t = _pack_w_in(w_in)
    p2, g2, vt, wi_t = _proj(x2, w_row, w_t, tm=tm, tk=tq)
    p3 = p2.reshape(b, s, P_COLS)
    g3 = g2.reshape(b, s, G_COLS)
    mix_a = _sb_attention(p3, g3, vt, tq)
    mix_b = _dsa_attention(p3, g3, vt, wi_t, rel_bias, tq, topk)
    out = _out_proj_ln(mix_a.reshape(b * s, WIDTH_A), mix_b.reshape(b * s, WIDTH_B), x2,
                       w_out.astype(BF16), ln_g.reshape(1, d), ln_b.reshape(1, d), alpha, tm=tm)
    return out.reshape(b, s, d)


def kernel(x, w_in, w_out, ln_g, ln_b, rel_bias):
    depth = w_in.shape[0]
    alpha = (2.0 * depth) ** 0.25
    topk = min(TOPK_MAX, x.shape[1] // 4)
    h = x
    for layer in range(depth):
        h = _layer(h, w_in[layer], w_out[layer], ln_g[layer], ln_b[layer], rel_bias, alpha, topk)
    return h
```

```python
import functools
import math

import jax
import jax.numpy as jnp
import numpy as np
from jax import lax
from jax.experimental import pallas as pl
from jax.experimental.pallas import tpu as pltpu

HEAD_DIM = 64
HEADS_A = 8
HEADS_B = 8
WIDTH_A = HEADS_A * HEAD_DIM
WIDTH_B = HEADS_B * HEAD_DIM
IDX_HEADS = 8
IDX_DIM = 64
CHUNK = 64
TOPK_MAX = 256
N_BUCKETS = 32
MAX_DISTANCE = 128
LN_EPS = 1e-5

LANES = 128
SUBLANES = 8
HEADS_PER_TILE = LANES // HEAD_DIM
VMEM_LIMIT_BYTES = 56 * 1024 * 1024

P_QA, P_KA, P_QB, P_KB, P_QI = (i * 512 for i in range(5))
P_KI = 5 * 512
P_COLS = P_KI + LANES
G_GA, G_GB = 0, 512
G_COLS = 1024
T_VA, T_VB, T_WI = 0, 512, 1024
T_WI_ROWS = 16
T_ROWS = T_WI + T_WI_ROWS

NEG_MASK = -1e30
MOST_NEGATIVE = float(np.finfo(np.float32).min)
NEG_INF_KEY16 = -32641
NO_CUT = 2 ** 30
HALF16 = 2 ** 15
SOFTMAX_SLACK = 40.0
LOG2E = math.log2(math.e)
DENOM_ROWS = 16
FAR_GROUP = 4
SB_HEADS_PER_STEP = 8
LOG2_DEAD = -151.0

BF16 = jnp.bfloat16
F32 = jnp.float32


def _dot_nt(a, b):
    return lax.dot_general(a, b, (((1,), (1,)), ((), ())), preferred_element_type=F32)


def _dot(a, b):
    return jnp.dot(a, b, preferred_element_type=F32)


def _mask_head(x, parity):
    lane = lax.broadcasted_iota(jnp.int32, x.shape, 1)
    keep = (lane < HEAD_DIM) if parity == 0 else (lane >= HEAD_DIM)
    return jnp.where(keep, x, jnp.zeros_like(x))


def _tree_sum(xs):
    while len(xs) > 1:
        xs = [a + b for a, b in zip(xs[::2], xs[1::2])] + ([xs[-1]] if len(xs) % 2 else [])
    return xs[0]


def _silu(g):
    return g / (1.0 + jnp.exp(-g))


def _proj_kernel(x_ref, wr_ref, wt_ref, p_ref, g_ref, vt_ref, wi_ref, *, tk):
    xb = x_ref[...].astype(BF16)
    for c in range(0, P_COLS, 512):
        cw = min(512, P_COLS - c)
        p_ref[:, c:c + cw] = _dot(xb, wr_ref[:, c:c + cw]).astype(BF16)
    for c in range(0, G_COLS, 512):
        g_ref[:, c:c + 512] = _dot(xb, wr_ref[:, P_COLS + c:P_COLS + c + 512])
    for r in range(0, T_WI, 512):
        t = _dot_nt(wt_ref[r:r + 512, :], xb).astype(BF16)
        for kb in range(xb.shape[0] // tk):
            vt_ref[kb, r:r + 512, :] = t[:, kb * tk:(kb + 1) * tk]
    wi_ref[...] = _dot_nt(wt_ref[T_WI:, :], xb)


def _proj(x2, w_row, w_t, tm, tk):
    m, d = x2.shape
    return pl.pallas_call(
        functools.partial(_proj_kernel, tk=tk),
        out_shape=(jax.ShapeDtypeStruct((m, P_COLS), BF16),
                   jax.ShapeDtypeStruct((m, G_COLS), F32),
                   jax.ShapeDtypeStruct((m // tk, T_WI, tk), BF16),
                   jax.ShapeDtypeStruct((T_WI_ROWS, m), F32)),
        grid=(m // tm,),
        in_specs=[pl.BlockSpec((tm, d), lambda i: (i, 0)),
                  pl.BlockSpec((d, P_COLS + G_COLS), lambda i: (0, 0)),
                  pl.BlockSpec((T_ROWS, d), lambda i: (0, 0))],
        out_specs=(pl.BlockSpec((tm, P_COLS), lambda i: (i, 0)),
                   pl.BlockSpec((tm, G_COLS), lambda i: (i, 0)),
                   pl.BlockSpec((tm // tk, T_WI, tk), lambda i: (i, 0, 0)),
                   pl.BlockSpec((T_WI_ROWS, tm), lambda i: (0, i))),
        compiler_params=pltpu.CompilerParams(
            dimension_semantics=("arbitrary",), vmem_limit_bytes=VMEM_LIMIT_BYTES),
        name="proj",
    )(x2, w_row, w_t)


def _softplus2(z):
    return jnp.maximum(z, 0.0) + jnp.log2(1.0 + jnp.exp2(-jnp.abs(z)))


def _sb_blocks(chains, u2):
    half = u2.shape[0]
    stricts = [c[3] for c in chains]
    masked = lambda xs: [x if m is None else jnp.where(m, x, 0.0) for x, m in zip(xs, stricts)]
    zts = [_dot_nt(kblk, qh) for qh, kblk, _, _ in chains]
    sps = masked([_softplus2(zt) for zt in zts])
    his = [sp.astype(BF16) for sp in sps]
    los = [(sp - hi.astype(F32)).astype(BF16) for sp, hi in zip(sps, his)]
    halves = lambda x: (x[:half], x[half:])
    tails = []
    for sp, hi, lo in zip(sps, his, los):
        (hi_a, hi_b), (lo_a, lo_b) = halves(hi), halves(lo)
        tail_a = _dot(u2, jnp.concatenate([hi_a, lo_a], axis=0))
        tail_b = _dot(u2, jnp.concatenate([hi_b, lo_b], axis=0))
        total_b = tail_b[:1, :] - sp[half:half + 1, :]
        tails.append(jnp.concatenate([tail_a + total_b, tail_b], axis=0))
    ws = masked([jnp.exp2(zt - sp + tail) for zt, sp, tail in zip(zts, sps, tails)])
    pvs = [_dot(c[2], a.astype(BF16)) for c, a in zip(chains, ws)]
    totals = [tail[:1, :] - sp[:1, :] for tail, sp in zip(tails, sps)]
    return list(zip(pvs, totals))


def _sb_kernel(q_ref, k_ref, vt_ref, u_ref, g_ref, o_ref, *, tq, n_heads):
    qi = pl.program_id(2)
    pair = lambda h: slice((h // HEADS_PER_TILE) * LANES, (h // HEADS_PER_TILE + 1) * LANES)
    qh = [_mask_head(q_ref[:, pair(h)], h % HEADS_PER_TILE) for h in range(n_heads)]
    u2 = u_ref[...]
    key_i = lax.broadcasted_iota(jnp.int32, (tq, tq), 0)
    qry_i = lax.broadcasted_iota(jnp.int32, (tq, tq), 1)
    strict = key_i < qry_i

    def chains(j, mask):
        start = pl.multiple_of(j * tq, tq)
        return [(qh[h], k_ref[pl.ds(start, tq), pair(h)], vt_ref[j, pair(h), :], mask)
                for h in range(n_heads)]

    has_prev = qi > 0
    res = _sb_blocks(chains(qi, strict) + chains(jnp.maximum(qi - 1, 0), None), u2)
    state = []
    for h in range(n_heads):
        (pv_d, tot_d), (pv_p, tot_p) = res[h], res[n_heads + h]
        state += [pv_d + jnp.where(has_prev, jnp.exp2(tot_d), 0.0) * pv_p, tot_d + tot_p]

    def alive_after(st):
        top = functools.reduce(jnp.maximum, st[1::2])
        return jnp.max(top) > LOG2_DEAD

    def cond(carry):
        j, alive = carry[0], carry[1]
        return jnp.logical_and(j >= 0, alive)

    def body(carry):
        j, st = carry[0], carry[2:]
        res = _sb_blocks(chains(j, None), u2)
        out = []
        for h in range(n_heads):
            acc, run = st[2 * h], st[2 * h + 1]
            pv, total = res[h]
            out += [acc + jnp.exp2(run) * pv, run + total]
        return (j - 1, alive_after(out)) + tuple(out)

    fin = lax.while_loop(cond, body, (qi - 2, alive_after(state)) + tuple(state))[2:]
    row = lax.broadcasted_iota(jnp.int32, fin[0].shape, 0)
    for hp in range(n_heads // HEADS_PER_TILE):
        acc0, acc1 = fin[2 * (2 * hp)], fin[2 * (2 * hp + 1)]
        ot = jnp.where(row < HEAD_DIM, acc0, acc1)
        o_ref[:, pair(2 * hp)] = (ot.T * _silu(g_ref[:, pair(2 * hp)])).astype(o_ref.dtype)


def _sb_attention(p3, g3, vt, tq):
    b, s, _ = p3.shape
    nkb = s // tq
    half = tq // 2
    tri = np.triu(-np.ones((half, half), np.float32), 1)
    u2 = jnp.asarray(np.concatenate([tri, tri], axis=1), BF16)
    w = SB_HEADS_PER_STEP * HEAD_DIM
    return pl.pallas_call(
        functools.partial(_sb_kernel, tq=tq, n_heads=SB_HEADS_PER_STEP),
        out_shape=jax.ShapeDtypeStruct((b, s, WIDTH_A), BF16),
        grid=(b, HEADS_A // SB_HEADS_PER_STEP, nkb),
        in_specs=[
            pl.BlockSpec((None, tq, w), lambda bi, hp, qi: (bi, qi, P_QA // w + hp)),
            pl.BlockSpec((None, s, w), lambda bi, hp, qi: (bi, 0, P_KA // w + hp)),
            pl.BlockSpec((nkb, w, tq), lambda bi, hp, qi: (bi, T_VA // w + hp, 0)),
            pl.BlockSpec((half, 2 * half), lambda bi, hp, qi: (0, 0)),
            pl.BlockSpec((None, tq, w), lambda bi, hp, qi: (bi, qi, G_GA // w + hp)),
        ],
        out_specs=pl.BlockSpec((None, tq, w), lambda bi, hp, qi: (bi, qi, hp)),
        compiler_params=pltpu.CompilerParams(
            dimension_semantics=("arbitrary", "arbitrary", "arbitrary"),
            vmem_limit_bytes=VMEM_LIMIT_BYTES),
        name="sb_attention",
    )(p3, p3, vt, u2, g3)


def _dsa_kernel(rb_ref, far_ref, qi_ref, wi_ref, ki_ref, q_ref, k_ref, vt_ref, bkt_ref, g_ref,
                o_ref, sc_ref, hi_ref, bias_ref, m_ref, acc_ref, *, tq, topk, seq):
    qi = pl.program_id(1)
    nblk = qi + 1

    @pl.when(jnp.logical_and(pl.program_id(0) == 0, qi == 0))
    def _():
        far = far_ref[0]
        for h in range(HEADS_B):
            for d in range(2):
                bkt = bkt_ref[d]

                def bucket_body(b, tile, bkt=bkt, h=h):
                    return jnp.where(bkt == b, (rb_ref[b, h] - rb_ref[far, h]) * LOG2E, tile)

                bias_ref[h, d] = lax.fori_loop(0, N_BUCKETS, bucket_body, jnp.zeros((tq, tq), F32))

    wi = wi_ref[...]
    qih = [_mask_head(qi_ref[:, (h // 2) * LANES:(h // 2 + 1) * LANES], h % 2)
           for h in range(IDX_HEADS)]

    def score_block(j, nb=1):
        start = pl.multiple_of(j * tq, tq)
        kib = ki_ref[pl.ds(start, tq * nb), :]
        rels = [_dot_nt(kib, qih[h]) for h in range(IDX_HEADS)]
        sc = jnp.zeros((tq * nb, tq), F32)
        for h in range(IDX_HEADS):
            sc = sc + jnp.maximum(rels[h], 0.0) * wi[h:h + 1, :]
        return sc

    def store_scores(j, sc):
        sc_ref[j] = sc
        hi_ref[j] = sc.astype(BF16)

    def p1_body(i, c):
        sc = score_block(2 * i, 2)
        store_scores(2 * i, sc[:tq])
        store_scores(2 * i + 1, sc[tq:])
        return c

    lax.fori_loop(0, qi // 2, p1_body, 0)

    @pl.when(qi % 2 == 1)
    def _():
        store_scores(qi - 1, score_block(qi - 1))

    key_i = lax.broadcasted_iota(jnp.int32, (tq, tq), 0)
    qry_i = lax.broadcasted_iota(jnp.int32, (tq, tq), 1)
    admissible = key_i < (qry_i // CHUNK + 1) * CHUNK
    store_scores(qi, jnp.where(admissible, score_block(qi), -jnp.inf))

    def count(pred):
        def body(j, acc):
            ones = jnp.where(pred(sc_ref[j], j * tq), 1.0, 0.0)
            return acc + _tree_sum([ones[r:r + SUBLANES, :] for r in range(0, tq, SUBLANES)])
        acc = lax.fori_loop(0, nblk, body, jnp.zeros((SUBLANES, tq), F32))
        return jnp.sum(acc, axis=0, keepdims=True)

    def count_hi(cand):
        rows = 2 * SUBLANES
        def body(i, acc):
            for blk in (hi_ref[2 * i], hi_ref[2 * i + 1]):
                ones = jnp.where(blk >= cand, jnp.int16(1), jnp.int16(0))
                acc = acc + _tree_sum([ones[r:r + rows, :] for r in range(0, tq, rows)])
            return acc
        acc = lax.fori_loop(0, (nblk + 1) // 2, body, jnp.zeros((rows, tq), jnp.int16))
        return jnp.sum(acc.astype(F32), axis=0, keepdims=True)

    @pl.when(nblk % 2 == 1)
    def _():
        hi_ref[nblk] = jnp.full((tq, tq), -jnp.inf, BF16)

    def float_of_key(key):
        return lax.bitcast_convert_type(jnp.where(key < 0, key ^ 0x7FFFFFFF, key), F32)

    def key32_of_key16(key16):
        return key16 * (2 * HALF16) + jnp.where(key16 < 0, 2 * HALF16 - 1, 0)

    scanned = (nblk * tq).astype(F32)
    kf = float(topk)
    assert seq // (2 * SUBLANES) < 2 ** 15

    def hi_body(i, prefix):
        cand_u = prefix | jnp.left_shift(jnp.int32(1), 15 - i)
        key16 = cand_u - HALF16
        cand = float_of_key(key32_of_key16(key16)).astype(BF16)
        cnt = jnp.where(key16 <= NEG_INF_KEY16, scanned, count_hi(cand))
        return jnp.where(cnt >= kf, cand_u, prefix)

    hi_key = lax.fori_loop(0, 16, hi_body, jnp.zeros((1, tq), jnp.int32)) - HALF16
    finite_thr = hi_key > NEG_INF_KEY16
    base = jnp.where(finite_thr, key32_of_key16(hi_key) - HALF16, 0)
    base_val = float_of_key(base)
    base_cnt = count(lambda blk, k0: blk >= base_val)

    def lo_body(i, carry):
        off, cnt_ge = carry
        off_try = off | jnp.left_shift(jnp.int32(1), 16 - i)
        cand = float_of_key(base + off_try)
        cnt = count(lambda blk, k0: blk >= cand)
        take = cnt >= kf
        return jnp.where(take, off_try, off), jnp.where(take, cnt, cnt_ge)

    off, cnt_ge = lax.fori_loop(0, 17, lo_body, (jnp.zeros((1, tq), jnp.int32), base_cnt))
    thr = float_of_key(base + off)

    tie = jnp.logical_and(cnt_ge > kf, finite_thr)
    has_tie = jnp.max(jnp.where(tie, 1.0, 0.0)) > 0.0

    @pl.when(jnp.logical_not(has_tie))
    def _():
        thr_min = jnp.where(finite_thr, thr, MOST_NEGATIVE)

        def mask_body(j, c):
            sc_ref[j] = jnp.where(sc_ref[j] >= thr_min, 0.0, NEG_MASK)
            return c

        lax.fori_loop(0, nblk, mask_body, 0)

    @pl.when(has_tie)
    def _():
        cnt_gt = count(lambda blk, k0: blk > thr)
        need = kf - cnt_gt
        nbits = int(seq).bit_length()

        def idx_body(i, lo):
            c_try = lo + jnp.left_shift(jnp.int32(1), nbits - 1 - i)
            f = count(lambda blk, k0: jnp.logical_and(blk == thr, (key_i + k0) < c_try))
            return jnp.where(f < need, c_try, lo)

        lo = lax.fori_loop(0, nbits, idx_body, jnp.zeros((1, tq), jnp.int32))
        cut = jnp.where(tie, lo + 1, jnp.where(finite_thr, NO_CUT, 0))
        thr_gt = jnp.where(finite_thr, thr, -jnp.inf)

        def mask_body(j, c):
            blk = sc_ref[j]
            at_thr = jnp.where((key_i + j * tq) < cut, 0.0, NEG_MASK)
            sc_ref[j] = jnp.where(blk > thr_gt, 0.0, jnp.where(blk == thr_gt, at_thr, NEG_MASK))
            return c

        lax.fori_loop(0, nblk, mask_body, 0)

    qh = [_mask_head(q_ref[:, (h // 2) * LANES:(h // 2 + 1) * LANES], h % 2)
          for h in range(HEADS_B)]
    m_ref[...] = jnp.full(m_ref.shape, NEG_MASK, F32)
    acc_ref[...] = jnp.zeros(acc_ref.shape, F32)

    def att_block(j, nb, d, stream=False):
        start = pl.multiple_of(j * tq, tq)
        cat = lambda xs, axis: xs[0] if nb == 1 else jnp.concatenate(xs, axis=axis)
        mask = cat([sc_ref[j + i] for i in range(nb)], 0)
        ones = jnp.ones((DENOM_ROWS, tq * nb), BF16)
        vts = [jnp.concatenate(
            [cat([vt_ref[j + i, h * HEAD_DIM:(h + 1) * HEAD_DIM, :] for i in range(nb)], 1), ones], axis=0)
            for h in range(HEADS_B)]
        pair = lambda h: slice((h // HEADS_PER_TILE) * LANES, (h // HEADS_PER_TILE + 1) * LANES)
        qk = lambda h: _dot_nt(k_ref[pl.ds(start, tq * nb), pair(h)], qh[h])
        m_old = m_ref[...]
        hs = range(HEADS_B)
        if stream:
            logits = [qk(h) + mask for h in hs]
            ps = [jnp.exp2(logits[h] - m_old[h:h + 1, :]).astype(BF16) for h in hs]
            bmax = jnp.concatenate([jnp.max(logits[h], axis=0, keepdims=True) for h in hs], axis=0)
            pvs = [_dot(vts[h], ps[h]) for h in hs]
            ok = jnp.max(bmax - m_old) <= SOFTMAX_SLACK

            @pl.when(ok)
            def _():
                for h in hs:
                    acc_ref[h] = acc_ref[h] + pvs[h]

            return ok
        logits = [qk(h) + mask for h in hs]
        if d is not None:
            logits = [lg + bias_ref[h, d] for lg, h in zip(logits, hs)]
        m_new = [jnp.maximum(m_old[h:h + 1, :], jnp.max(logits[h], axis=0, keepdims=True)) for h in hs]
        ps = [jnp.exp2(logits[h] - m_new[h]).astype(BF16) for h in hs]
        pvs = [_dot(vts[h], ps[h]) for h in hs]
        accs = [jnp.exp2(m_old[h:h + 1, :] - m_new[h]) * acc_ref[h] + pvs[h] for h in hs]
        m_ref[...] = jnp.concatenate(m_new, axis=0)
        for h in range(HEADS_B):
            acc_ref[h] = accs[h]

    att_block(qi, 1, 0)

    @pl.when(qi > 0)
    def _():
        att_block(qi - 1, 1, 1)

    n_far = qi - 1

    def far_blocks(j, nb):
        ok = att_block(j, nb, None, stream=True)

        @pl.when(jnp.logical_not(ok))
        def _():
            att_block(j, nb, None)

    def far_body(i, c):
        far_blocks(FAR_GROUP * i, FAR_GROUP)
        return c

    n_groups = jnp.maximum(n_far, 0) // FAR_GROUP
    lax.fori_loop(0, n_groups, far_body, 0)
    done = n_groups * FAR_GROUP
    size = FAR_GROUP // 2
    while size >= 1:
        take = jnp.logical_and(n_far > 0, (n_far // size) % 2 == 1)

        @pl.when(take)
        def _(done=done, size=size):
            far_blocks(done, size)

        done = done + jnp.where(take, size, 0)
        size //= 2

    for hp in range(HEADS_B // HEADS_PER_TILE):
        h0 = hp * HEADS_PER_TILE
        a0, a1 = acc_ref[h0], acc_ref[h0 + 1]
        ot = jnp.concatenate([a0[:HEAD_DIM] / a0[HEAD_DIM:HEAD_DIM + 1, :],
                              a1[:HEAD_DIM] / a1[HEAD_DIM:HEAD_DIM + 1, :]], axis=0)
        gp = g_ref[:, hp * LANES:(hp + 1) * LANES]
        o_ref[:, hp * LANES:(hp + 1) * LANES] = (ot.T * _silu(gp)).astype(o_ref.dtype)


def _t5_bucket(rel):
    half = N_BUCKETS // 2
    max_exact = half // 2
    ret = jnp.where(rel > 0, half, 0)
    n = jnp.abs(rel)
    nf = jnp.maximum(n, 1).astype(jnp.float32)
    large = max_exact + (jnp.log(nf / max_exact) / math.log(MAX_DISTANCE / max_exact)
                         * (half - max_exact)).astype(jnp.int32)
    large = jnp.minimum(large, half - 1)
    return ret + jnp.where(n < max_exact, n, large)


def _dsa_attention(p3, g3, vt, wi_t, rel_bias, tq, topk):
    b, s, _ = p3.shape
    nkb = s // tq
    assert tq >= MAX_DISTANCE
    key = jnp.arange(tq, dtype=jnp.int32)[:, None]
    qry = jnp.arange(tq, dtype=jnp.int32)[None, :]
    buckets = jnp.stack([_t5_bucket(key - qry), _t5_bucket(key - qry - tq)])
    far = _t5_bucket(jnp.full((1,), -tq - 1, jnp.int32))
    buckets = jnp.bitwise_and(buckets, 2 * N_BUCKETS - 1)
    far = jnp.bitwise_and(far, 2 * N_BUCKETS - 1)
    row = lambda blk: (lambda bi, qi: (bi, qi, blk))
    full = lambda blk: (lambda bi, qi: (bi, 0, blk))
    smem = pl.BlockSpec(memory_space=pltpu.SMEM)
    return pl.pallas_call(
        functools.partial(_dsa_kernel, tq=tq, topk=topk, seq=s),
        out_shape=jax.ShapeDtypeStruct((b, s, WIDTH_B), BF16),
        grid=(b, nkb),
        in_specs=[
            smem, smem,
            pl.BlockSpec((None, tq, 512), row(P_QI // 512)),
            pl.BlockSpec((T_WI_ROWS, tq), lambda bi, qi: (0, bi * nkb + qi)),
            pl.BlockSpec((None, s, LANES), full(P_KI // LANES)),
            pl.BlockSpec((None, tq, 512), row(P_QB // 512)),
            pl.BlockSpec((None, s, 512), full(P_KB // 512)),
            pl.BlockSpec((nkb, WIDTH_B, tq), lambda bi, qi: (bi, T_VB // WIDTH_B, 0)),
            pl.BlockSpec((2, tq, tq), lambda bi, qi: (0, 0, 0)),
            pl.BlockSpec((None, tq, 512), row(G_GB // 512)),
        ],
        out_specs=pl.BlockSpec((None, tq, WIDTH_B), lambda bi, qi: (bi, qi, 0)),
        scratch_shapes=[pltpu.VMEM((nkb, tq, tq), F32),
                        pltpu.VMEM((nkb + nkb % 2, tq, tq), BF16),
                        pltpu.VMEM((HEADS_B, 2, tq, tq), F32),
                        pltpu.VMEM((HEADS_B, tq), F32),
                        pltpu.VMEM((HEADS_B, HEAD_DIM + DENOM_ROWS, tq), F32)],
        compiler_params=pltpu.CompilerParams(
            dimension_semantics=("arbitrary", "arbitrary"),
            vmem_limit_bytes=VMEM_LIMIT_BYTES),
        name="dsa_attention",
    )(rel_bias, far, p3, wi_t, p3, p3, p3, vt, buckets, g3)


def _out_kernel(ma_ref, mb_ref, x_ref, w_ref, lg_ref, lb_ref, o_ref, *, alpha):
    y = _dot(ma_ref[...], w_ref[:WIDTH_A, :]) + _dot(mb_ref[...], w_ref[WIDTH_A:, :])
    h = alpha * x_ref[...] + y
    mu = jnp.mean(h, axis=-1, keepdims=True)
    hc = h - mu
    var = jnp.mean(hc * hc, axis=-1, keepdims=True)
    o_ref[...] = hc * lax.rsqrt(var + LN_EPS) * lg_ref[...] + lb_ref[...]


def _out_proj_ln(mix_a, mix_b, x2, w_out, ln_g, ln_b, alpha, tm):
    m, d = x2.shape
    return pl.pallas_call(
        functools.partial(_out_kernel, alpha=alpha),
        out_shape=jax.ShapeDtypeStruct((m, d), F32),
        grid=(m // tm,),
        in_specs=[pl.BlockSpec((tm, WIDTH_A), lambda i: (i, 0)),
                  pl.BlockSpec((tm, WIDTH_B), lambda i: (i, 0)),
                  pl.BlockSpec((tm, d), lambda i: (i, 0)),
                  pl.BlockSpec((WIDTH_A + WIDTH_B, d), lambda i: (0, 0)),
                  pl.BlockSpec((1, d), lambda i: (0, 0)),
                  pl.BlockSpec((1, d), lambda i: (0, 0))],
        out_specs=pl.BlockSpec((tm, d), lambda i: (i, 0)),
        compiler_params=pltpu.CompilerParams(
            dimension_semantics=("arbitrary",), vmem_limit_bytes=VMEM_LIMIT_BYTES),
        name="out_proj_ln",
    )(mix_a, mix_b, x2, w_out, ln_g, ln_b)


def _pack_w_in(w):
    scale = LOG2E / math.sqrt(HEAD_DIM)
    sizes = (WIDTH_A,) * 4 + (WIDTH_B,) * 4 + (IDX_HEADS * IDX_DIM, IDX_DIM, IDX_HEADS)
    qa, ka, va, ga, qb, kb, vb, gb, qi, ki, wi = jnp.split(w, np.cumsum(sizes)[:-1].tolist(), axis=1)
    w_row = jnp.concatenate([qa * scale, ka, qb * scale, kb, qi, ki, ki, ga, gb], axis=1)
    wi_pad = jnp.pad(wi, ((0, 0), (0, T_WI_ROWS - IDX_HEADS)))
    w_t = jnp.concatenate([va, vb, wi_pad], axis=1).T
    return w_row.astype(BF16), w_t.astype(BF16)


def _layer(x, w_in, w_out, ln_g, ln_b, rel_bias, alpha, topk):
    b, s, d = x.shape
    tq = min(256, s)
    tm = min(512, b * s)
    x2 = x.reshape(b * s, d)
    w_row, w_t = _pack_w_in(w_in)
    p2, g2, vt, wi_t = _proj(x2, w_row, w_t, tm=tm, tk=tq)
    p3 = p2.reshape(b, s, P_COLS)
    g3 = g2.reshape(b, s, G_COLS)
    mix_a = _sb_attention(p3, g3, vt, tq)
    mix_b = _dsa_attention(p3, g3, vt, wi_t, rel_bias, tq, topk)
    out = _out_proj_ln(mix_a.reshape(b * s, WIDTH_A), mix_b.reshape(b * s, WIDTH_B), x2,
                       w_out.astype(BF16), ln_g.reshape(1, d), ln_b.reshape(1, d), alpha, tm=tm)
    return out.reshape(b, s, d)


def kernel(x, w_in, w_out, ln_g, ln_b, rel_bias):
    depth = w_in.shape[0]
    alpha = (2.0 * depth) ** 0.25
    topk = min(TOPK_MAX, x.shape[1] // 4)
    h = x
    for layer in range(depth):
        h = _layer(h, w_in[layer], w_out[layer], ln_g[layer], ln_b[layer], rel_bias, alpha, topk)
    return h
```

```python
import functools
import math

import jax
import jax.numpy as jnp
import numpy as np
from jax import lax
from jax.experimental import pallas as pl
from jax.experimental.pallas import tpu as pltpu

HEAD_DIM = 64
HEADS_A = 8
HEADS_B = 8
WIDTH_A = HEADS_A * HEAD_DIM
WIDTH_B = HEADS_B * HEAD_DIM
IDX_HEADS = 8
IDX_DIM = 64
CHUNK = 64
TOPK_MAX = 256
N_BUCKETS = 32
MAX_DISTANCE = 128
LN_EPS = 1e-5

LANES = 128
SUBLANES = 8
HEADS_PER_TILE = LANES // HEAD_DIM
VMEM_LIMIT_BYTES = 56 * 1024 * 1024

P_QA, P_KA, P_QB, P_KB, P_QI = (i * 512 for i in range(5))
P_KI = 5 * 512
P_COLS = P_KI + LANES
G_GA, G_GB = 0, 512
G_COLS = 1024
T_VA, T_VB, T_WI = 0, 512, 1024
T_WI_ROWS = 16
T_ROWS = T_WI + T_WI_ROWS

NEG_MASK = -1e30
MOST_NEGATIVE = float(np.finfo(np.float32).min)
NEG_INF_KEY16 = -32641
NO_CUT = 2 ** 30
HALF16 = 2 ** 15
SOFTMAX_SLACK = 40.0
LOG2E = math.log2(math.e)
DENOM_ROWS = 16
P1_GROUP = 4
FAR_GROUP = 4
SB_HEADS_PER_STEP = 8
LOG2_DEAD = -151.0

BF16 = jnp.bfloat16
F32 = jnp.float32


def _dot_nt(a, b):
    return lax.dot_general(a, b, (((1,), (1,)), ((), ())), preferred_element_type=F32)


def _dot(a, b):
    return jnp.dot(a, b, preferred_element_type=F32)


def _mask_head(x, parity):
    lane = lax.broadcasted_iota(jnp.int32, x.shape, 1)
    keep = (lane < HEAD_DIM) if parity == 0 else (lane >= HEAD_DIM)
    return jnp.where(keep, x, jnp.zeros_like(x))


def _tree_sum(xs):
    while len(xs) > 1:
        xs = [a + b for a, b in zip(xs[::2], xs[1::2])] + ([xs[-1]] if len(xs) % 2 else [])
    return xs[0]


def _silu(g):
    return g / (1.0 + jnp.exp(-g))


def _proj_kernel(x_ref, wr_ref, wt_ref, p_ref, g_ref, vt_ref, wi_ref, *, tk):
    xb = x_ref[...].astype(BF16)
    for c in range(0, P_COLS, 512):
        cw = min(512, P_COLS - c)
        p_ref[:, c:c + cw] = _dot(xb, wr_ref[:, c:c + cw]).astype(BF16)
    for c in range(0, G_COLS, 512):
        g_ref[:, c:c + 512] = _dot(xb, wr_ref[:, P_COLS + c:P_COLS + c + 512])
    for r in range(0, T_WI, 512):
        t = _dot_nt(wt_ref[r:r + 512, :], xb).astype(BF16)
        for kb in range(xb.shape[0] // tk):
            vt_ref[kb, r:r + 512, :] = t[:, kb * tk:(kb + 1) * tk]
    wi_ref[...] = _dot_nt(wt_ref[T_WI:, :], xb)


def _proj(x2, w_row, w_t, tm, tk):
    m, d = x2.shape
    return pl.pallas_call(
        functools.partial(_proj_kernel, tk=tk),
        out_shape=(jax.ShapeDtypeStruct((m, P_COLS), BF16),
                   jax.ShapeDtypeStruct((m, G_COLS), F32),
                   jax.ShapeDtypeStruct((m // tk, T_WI, tk), BF16),
                   jax.ShapeDtypeStruct((T_WI_ROWS, m), F32)),
        grid=(m // tm,),
        in_specs=[pl.BlockSpec((tm, d), lambda i: (i, 0)),
                  pl.BlockSpec((d, P_COLS + G_COLS), lambda i: (0, 0)),
                  pl.BlockSpec((T_ROWS, d), lambda i: (0, 0))],
        out_specs=(pl.BlockSpec((tm, P_COLS), lambda i: (i, 0)),
                   pl.BlockSpec((tm, G_COLS), lambda i: (i, 0)),
                   pl.BlockSpec((tm // tk, T_WI, tk), lambda i: (i, 0, 0)),
                   pl.BlockSpec((T_WI_ROWS, tm), lambda i: (0, i))),
        compiler_params=pltpu.CompilerParams(
            dimension_semantics=("arbitrary",), vmem_limit_bytes=VMEM_LIMIT_BYTES),
        name="proj",
    )(x2, w_row, w_t)


def _softplus2(z):
    return jnp.maximum(z, 0.0) + jnp.log2(1.0 + jnp.exp2(-jnp.abs(z)))


def _sb_blocks(chains, u2):
    half = u2.shape[0]
    stricts = [c[3] for c in chains]
    masked = lambda xs: [x if m is None else jnp.where(m, x, 0.0) for x, m in zip(xs, stricts)]
    zts = [_dot_nt(kblk, qh) for qh, kblk, _, _ in chains]
    sps = masked([_softplus2(zt) for zt in zts])
    his = [sp.astype(BF16) for sp in sps]
    los = [(sp - hi.astype(F32)).astype(BF16) for sp, hi in zip(sps, his)]
    halves = lambda x: (x[:half], x[half:])
    tails = []
    for sp, hi, lo in zip(sps, his, los):
        (hi_a, hi_b), (lo_a, lo_b) = halves(hi), halves(lo)
        tail_a = _dot(u2, jnp.concatenate([hi_a, lo_a], axis=0))
        tail_b = _dot(u2, jnp.concatenate([hi_b, lo_b], axis=0))
        total_b = tail_b[:1, :] - sp[half:half + 1, :]
        tails.append(jnp.concatenate([tail_a + total_b, tail_b], axis=0))
    ws = masked([jnp.exp2(zt - sp + tail) for zt, sp, tail in zip(zts, sps, tails)])
    pvs = [_dot(c[2], a.astype(BF16)) for c, a in zip(chains, ws)]
    totals = [tail[:1, :] - sp[:1, :] for tail, sp in zip(tails, sps)]
    return list(zip(pvs, totals))


def _sb_kernel(q_ref, k_ref, vt_ref, u_ref, g_ref, o_ref, *, tq, n_heads):
    qi = pl.program_id(2)
    pair = lambda h: slice((h // HEADS_PER_TILE) * LANES, (h // HEADS_PER_TILE + 1) * LANES)
    qh = [_mask_head(q_ref[:, pair(h)], h % HEADS_PER_TILE) for h in range(n_heads)]
    u2 = u_ref[...]
    key_i = lax.broadcasted_iota(jnp.int32, (tq, tq), 0)
    qry_i = lax.broadcasted_iota(jnp.int32, (tq, tq), 1)
    strict = key_i < qry_i

    def chains(j, mask):
        start = pl.multiple_of(j * tq, tq)
        return [(qh[h], k_ref[pl.ds(start, tq), pair(h)], vt_ref[j, pair(h), :], mask)
                for h in range(n_heads)]

    has_prev = qi > 0
    res = _sb_blocks(chains(qi, strict) + chains(jnp.maximum(qi - 1, 0), None), u2)
    state = []
    for h in range(n_heads):
        (pv_d, tot_d), (pv_p, tot_p) = res[h], res[n_heads + h]
        state += [pv_d + jnp.where(has_prev, jnp.exp2(tot_d), 0.0) * pv_p, tot_d + tot_p]

    def alive_after(st):
        top = functools.reduce(jnp.maximum, st[1::2])
        return jnp.max(top) > LOG2_DEAD

    def cond(carry):
        j, alive = carry[0], carry[1]
        return jnp.logical_and(j >= 0, alive)

    def body(carry):
        j, st = carry[0], carry[2:]
        res = _sb_blocks(chains(j, None), u2)
        out = []
        for h in range(n_heads):
            acc, run = st[2 * h], st[2 * h + 1]
            pv, total = res[h]
            out += [acc + jnp.exp2(run) * pv, run + total]
        return (j - 1, alive_after(out)) + tuple(out)

    fin = lax.while_loop(cond, body, (qi - 2, alive_after(state)) + tuple(state))[2:]
    row = lax.broadcasted_iota(jnp.int32, fin[0].shape, 0)
    for hp in range(n_heads // HEADS_PER_TILE):
        acc0, acc1 = fin[2 * (2 * hp)], fin[2 * (2 * hp + 1)]
        ot = jnp.where(row < HEAD_DIM, acc0, acc1)
        o_ref[:, pair(2 * hp)] = (ot.T * _silu(g_ref[:, pair(2 * hp)])).astype(o_ref.dtype)


def _sb_attention(p3, g3, vt, tq):
    b, s, _ = p3.shape
    nkb = s // tq
    half = tq // 2
    tri = np.triu(-np.ones((half, half), np.float32), 1)
    u2 = jnp.asarray(np.concatenate([tri, tri], axis=1), BF16)
    w = SB_HEADS_PER_STEP * HEAD_DIM
    return pl.pallas_call(
        functools.partial(_sb_kernel, tq=tq, n_heads=SB_HEADS_PER_STEP),
        out_shape=jax.ShapeDtypeStruct((b, s, WIDTH_A), BF16),
        grid=(b, HEADS_A // SB_HEADS_PER_STEP, nkb),
        in_specs=[
            pl.BlockSpec((None, tq, w), lambda bi, hp, qi: (bi, qi, P_QA // w + hp)),
            pl.BlockSpec((None, s, w), lambda bi, hp, qi: (bi, 0, P_KA // w + hp)),
            pl.BlockSpec((nkb, w, tq), lambda bi, hp, qi: (bi, T_VA // w + hp, 0)),
            pl.BlockSpec((half, 2 * half), lambda bi, hp, qi: (0, 0)),
            pl.BlockSpec((None, tq, w), lambda bi, hp, qi: (bi, qi, G_GA // w + hp)),
        ],
        out_specs=pl.BlockSpec((None, tq, w), lambda bi, hp, qi: (bi, qi, hp)),
        compiler_params=pltpu.CompilerParams(
            dimension_semantics=("arbitrary", "arbitrary", "arbitrary"),
            vmem_limit_bytes=VMEM_LIMIT_BYTES),
        name="sb_attention",
    )(p3, p3, vt, u2, g3)


def _dsa_kernel(rb_ref, far_ref, qi_ref, wi_ref, ki_ref, q_ref, k_ref, vt_ref, bkt_ref, g_ref,
                o_ref, sc_ref, hi_ref, bias_ref, m_ref, acc_ref, *, tq, topk, seq):
    qi = pl.program_id(1)
    nblk = qi + 1

    @pl.when(jnp.logical_and(pl.program_id(0) == 0, qi == 0))
    def _():
        far = far_ref[0]
        for h in range(HEADS_B):
            for d in range(2):
                bkt = bkt_ref[d]

                def bucket_body(b, tile, bkt=bkt, h=h):
                    return jnp.where(bkt == b, (rb_ref[b, h] - rb_ref[far, h]) * LOG2E, tile)

                bias_ref[h, d] = lax.fori_loop(0, N_BUCKETS, bucket_body, jnp.zeros((tq, tq), F32))

    wi = wi_ref[...]
    qih = [_mask_head(qi_ref[:, (h // 2) * LANES:(h // 2 + 1) * LANES], h % 2)
           for h in range(IDX_HEADS)]

    def score_block(j, nb=1):
        start = pl.multiple_of(j * tq, tq)
        kib = ki_ref[pl.ds(start, tq * nb), :]
        rels = [_dot_nt(kib, qih[h]) for h in range(IDX_HEADS)]
        sc = jnp.zeros((tq * nb, tq), F32)
        for h in range(IDX_HEADS):
            sc = sc + jnp.maximum(rels[h], 0.0) * wi[h:h + 1, :]
        return sc

    def store_scores(j, sc):
        sc_ref[j] = sc
        hi_ref[j] = sc.astype(BF16)

    def score_group(j, nb):
        sc = score_block(j, nb)
        for i in range(nb):
            store_scores(j + i, sc[i * tq:(i + 1) * tq])

    def p1_body(i, c):
        score_group(P1_GROUP * i, P1_GROUP)
        return c

    n_groups1 = qi // P1_GROUP
    lax.fori_loop(0, n_groups1, p1_body, 0)
    done1 = n_groups1 * P1_GROUP
    size = P1_GROUP // 2
    while size >= 1:
        take = (qi // size) % 2 == 1

        @pl.when(take)
        def _(done1=done1, size=size):
            score_group(done1, size)

        done1 = done1 + jnp.where(take, size, 0)
        size //= 2

    key_i = lax.broadcasted_iota(jnp.int32, (tq, tq), 0)
    qry_i = lax.broadcasted_iota(jnp.int32, (tq, tq), 1)
    admissible = key_i < (qry_i // CHUNK + 1) * CHUNK
    store_scores(qi, jnp.where(admissible, score_block(qi), -jnp.inf))

    def count(pred):
        def body(i, acc):
            for j in (2 * i, 2 * i + 1):
                ones = jnp.where(pred(sc_ref[j], j * tq), 1.0, 0.0)
                acc = acc + _tree_sum([ones[r:r + SUBLANES, :] for r in range(0, tq, SUBLANES)])
            return acc
        acc = lax.fori_loop(0, (nblk + 1) // 2, body, jnp.zeros((SUBLANES, tq), F32))
        return jnp.sum(acc, axis=0, keepdims=True)

    def count_hi(cand):
        rows = 2 * SUBLANES
        def body(i, acc):
            for blk in (hi_ref[2 * i], hi_ref[2 * i + 1]):
                ones = jnp.where(blk >= cand, jnp.int16(1), jnp.int16(0))
                acc = acc + _tree_sum([ones[r:r + rows, :] for r in range(0, tq, rows)])
            return acc
        acc = lax.fori_loop(0, (nblk + 1) // 2, body, jnp.zeros((rows, tq), jnp.int16))
        return jnp.sum(acc.astype(F32), axis=0, keepdims=True)

    @pl.when(nblk % 2 == 1)
    def _():
        sc_ref[nblk] = jnp.full((tq, tq), -jnp.inf, F32)
        hi_ref[nblk] = jnp.full((tq, tq), -jnp.inf, BF16)

    def float_of_key(key):
        return lax.bitcast_convert_type(jnp.where(key < 0, key ^ 0x7FFFFFFF, key), F32)

    def key32_of_key16(key16):
        return key16 * (2 * HALF16) + jnp.where(key16 < 0, 2 * HALF16 - 1, 0)

    scanned = (nblk * tq).astype(F32)
    kf = float(topk)
    assert seq // (2 * SUBLANES) < 2 ** 15

    def hi_body(i, prefix):
        cand_u = prefix | jnp.left_shift(jnp.int32(1), 15 - i)
        key16 = cand_u - HALF16
        cand = float_of_key(key32_of_key16(key16)).astype(BF16)
        cnt = jnp.where(key16 <= NEG_INF_KEY16, scanned, count_hi(cand))
        return jnp.where(cnt >= kf, cand_u, prefix)

    hi_key = lax.fori_loop(0, 16, hi_body, jnp.zeros((1, tq), jnp.int32)) - HALF16
    finite_thr = hi_key > NEG_INF_KEY16
    base = jnp.where(finite_thr, key32_of_key16(hi_key) - HALF16, 0)
    base_val = float_of_key(base)
    base_cnt = count(lambda blk, k0: blk >= base_val)

    def lo_body(i, carry):
        off, cnt_ge = carry
        off_try = off | jnp.left_shift(jnp.int32(1), 16 - i)
        cand = float_of_key(base + off_try)
        cnt = count(lambda blk, k0: blk >= cand)
        take = cnt >= kf
        return jnp.where(take, off_try, off), jnp.where(take, cnt, cnt_ge)

    off, cnt_ge = lax.fori_loop(0, 17, lo_body, (jnp.zeros((1, tq), jnp.int32), base_cnt))
    thr = float_of_key(base + off)

    tie = jnp.logical_and(cnt_ge > kf, finite_thr)
    has_tie = jnp.max(jnp.where(tie, 1.0, 0.0)) > 0.0

    @pl.when(jnp.logical_not(has_tie))
    def _():
        thr_min = jnp.where(finite_thr, thr, MOST_NEGATIVE)

        def mask_body(j, c):
            sc_ref[j] = jnp.where(sc_ref[j] >= thr_min, 0.0, NEG_MASK)
            return c

        lax.fori_loop(0, nblk, mask_body, 0)

    @pl.when(has_tie)
    def _():
        cnt_gt = count(lambda blk, k0: blk > thr)
        need = kf - cnt_gt
        nbits = int(seq).bit_length()

        def idx_body(i, lo):
            c_try = lo + jnp.left_shift(jnp.int32(1), nbits - 1 - i)
            f = count(lambda blk, k0: jnp.logical_and(blk == thr, (key_i + k0) < c_try))
            return jnp.where(f < need, c_try, lo)

        lo = lax.fori_loop(0, nbits, idx_body, jnp.zeros((1, tq), jnp.int32))
        cut = jnp.where(tie, lo + 1, jnp.where(finite_thr, NO_CUT, 0))
        thr_gt = jnp.where(finite_thr, thr, -jnp.inf)

        def mask_body(j, c):
            blk = sc_ref[j]
            at_thr = jnp.where((key_i + j * tq) < cut, 0.0, NEG_MASK)
            sc_ref[j] = jnp.where(blk > thr_gt, 0.0, jnp.where(blk == thr_gt, at_thr, NEG_MASK))
            return c

        lax.fori_loop(0, nblk, mask_body, 0)

    qh = [_mask_head(q_ref[:, (h // 2) * LANES:(h // 2 + 1) * LANES], h % 2)
          for h in range(HEADS_B)]
    m_ref[...] = jnp.full(m_ref.shape, NEG_MASK, F32)
    acc_ref[...] = jnp.zeros(acc_ref.shape, F32)

    def att_block(j, nb, d, stream=False):
        start = pl.multiple_of(j * tq, tq)
        cat = lambda xs, axis: xs[0] if nb == 1 else jnp.concatenate(xs, axis=axis)
        mask = cat([sc_ref[j + i] for i in range(nb)], 0)
        ones = jnp.ones((DENOM_ROWS, tq * nb), BF16)
        vts = [jnp.concatenate(
            [cat([vt_ref[j + i, h * HEAD_DIM:(h + 1) * HEAD_DIM, :] for i in range(nb)], 1), ones], axis=0)
            for h in range(HEADS_B)]
        pair = lambda h: slice((h // HEADS_PER_TILE) * LANES, (h // HEADS_PER_TILE + 1) * LANES)
        qk = lambda h: _dot_nt(k_ref[pl.ds(start, tq * nb), pair(h)], qh[h])
        m_old = m_ref[...]
        hs = range(HEADS_B)
        logits = [qk(h) + mask for h in hs]
        if d is not None:
            logits = [lg + bias_ref[h, d] for lg, h in zip(logits, hs)]
        if stream:
            ps = [jnp.exp2(logits[h] - m_old[h:h + 1, :]).astype(BF16) for h in hs]
            bmax = jnp.concatenate([jnp.max(logits[h], axis=0, keepdims=True) for h in hs], axis=0)
            pvs = [_dot(vts[h], ps[h]) for h in hs]
            ok = jnp.max(bmax - m_old) <= SOFTMAX_SLACK

            @pl.when(ok)
            def _():
                for h in hs:
                    acc_ref[h] = acc_ref[h] + pvs[h]

            return ok
        m_new = [jnp.maximum(m_old[h:h + 1, :], jnp.max(logits[h], axis=0, keepdims=True)) for h in hs]
        ps = [jnp.exp2(logits[h] - m_new[h]).astype(BF16) for h in hs]
        pvs = [_dot(vts[h], ps[h]) for h in hs]
        accs = [jnp.exp2(m_old[h:h + 1, :] - m_new[h]) * acc_ref[h] + pvs[h] for h in hs]
        m_ref[...] = jnp.concatenate(m_new, axis=0)
        for h in range(HEADS_B):
            acc_ref[h] = accs[h]

    def single_pass_or_exact(j, nb, d):
        ok = att_block(j, nb, d, stream=True)

        @pl.when(jnp.logical_not(ok))
        def _():
            att_block(j, nb, d)

    att_block(qi, 1, 0)

    @pl.when(qi > 0)
    def _():
        single_pass_or_exact(qi - 1, 1, 1)

    n_far = qi - 1
    far_blocks = lambda j, nb: single_pass_or_exact(j, nb, None)

    def far_body(i, c):
        far_blocks(FAR_GROUP * i, FAR_GROUP)
        return c

    n_groups = jnp.maximum(n_far, 0) // FAR_GROUP
    lax.fori_loop(0, n_groups, far_body, 0)
    done = n_groups * FAR_GROUP
    size = FAR_GROUP // 2
    while size >= 1:
        take = jnp.logical_and(n_far > 0, (n_far // size) % 2 == 1)

        @pl.when(take)
        def _(done=done, size=size):
            far_blocks(done, size)

        done = done + jnp.where(take, size, 0)
        size //= 2

    for hp in range(HEADS_B // HEADS_PER_TILE):
        h0 = hp * HEADS_PER_TILE
        a0, a1 = acc_ref[h0], acc_ref[h0 + 1]
        ot = jnp.concatenate([a0[:HEAD_DIM] / a0[HEAD_DIM:HEAD_DIM + 1, :],
                              a1[:HEAD_DIM] / a1[HEAD_DIM:HEAD_DIM + 1, :]], axis=0)
        gp = g_ref[:, hp * LANES:(hp + 1) * LANES]
        o_ref[:, hp * LANES:(hp + 1) * LANES] = (ot.T * _silu(gp)).astype(o_ref.dtype)


def _t5_bucket(rel):
    half = N_BUCKETS // 2
    max_exact = half // 2
    ret = jnp.where(rel > 0, half, 0)
    n = jnp.abs(rel)
    nf = jnp.maximum(n, 1).astype(jnp.float32)
    large = max_exact + (jnp.log(nf / max_exact) / math.log(MAX_DISTANCE / max_exact)
                         * (half - max_exact)).astype(jnp.int32)
    large = jnp.minimum(large, half - 1)
    return ret + jnp.where(n < max_exact, n, large)


def _dsa_attention(p3, g3, vt, wi_t, rel_bias, tq, topk):
    b, s, _ = p3.shape
    nkb = s // tq
    assert tq >= MAX_DISTANCE
    key = jnp.arange(tq, dtype=jnp.int32)[:, None]
    qry = jnp.arange(tq, dtype=jnp.int32)[None, :]
    buckets = jnp.stack([_t5_bucket(key - qry), _t5_bucket(key - qry - tq)])
    far = _t5_bucket(jnp.full((1,), -tq - 1, jnp.int32))
    buckets = jnp.bitwise_and(buckets, 2 * N_BUCKETS - 1)
    far = jnp.bitwise_and(far, 2 * N_BUCKETS - 1)
    row = lambda blk: (lambda bi, qi: (bi, qi, blk))
    full = lambda blk: (lambda bi, qi: (bi, 0, blk))
    smem = pl.BlockSpec(memory_space=pltpu.SMEM)
    return pl.pallas_call(
        functools.partial(_dsa_kernel, tq=tq, topk=topk, seq=s),
        out_shape=jax.ShapeDtypeStruct((b, s, WIDTH_B), BF16),
        grid=(b, nkb),
        in_specs=[
            smem, smem,
            pl.BlockSpec((None, tq, 512), row(P_QI // 512)),
            pl.BlockSpec((T_WI_ROWS, tq), lambda bi, qi: (0, bi * nkb + qi)),
            pl.BlockSpec((None, s, LANES), full(P_KI // LANES)),
            pl.BlockSpec((None, tq, 512), row(P_QB // 512)),
            pl.BlockSpec((None, s, 512), full(P_KB // 512)),
            pl.BlockSpec((nkb, WIDTH_B, tq), lambda bi, qi: (bi, T_VB // WIDTH_B, 0)),
            pl.BlockSpec((2, tq, tq), lambda bi, qi: (0, 0, 0)),
            pl.BlockSpec((None, tq, 512), row(G_GB // 512)),
        ],
        out_specs=pl.BlockSpec((None, tq, WIDTH_B), lambda bi, qi: (bi, qi, 0)),
        scratch_shapes=[pltpu.VMEM((nkb + nkb % 2, tq, tq), F32),
                        pltpu.VMEM((nkb + nkb % 2, tq, tq), BF16),
                        pltpu.VMEM((HEADS_B, 2, tq, tq), F32),
                        pltpu.VMEM((HEADS_B, tq), F32),
                        pltpu.VMEM((HEADS_B, HEAD_DIM + DENOM_ROWS, tq), F32)],
        compiler_params=pltpu.CompilerParams(
            dimension_semantics=("arbitrary", "arbitrary"),
            vmem_limit_bytes=VMEM_LIMIT_BYTES),
        name="dsa_attention",
    )(rel_bias, far, p3, wi_t, p3, p3, p3, vt, buckets, g3)


def _out_kernel(ma_ref, mb_ref, x_ref, w_ref, lg_ref, lb_ref, o_ref, *, alpha):
    y = _dot(ma_ref[...], w_ref[:WIDTH_A, :]) + _dot(mb_ref[...], w_ref[WIDTH_A:, :])
    h = alpha * x_ref[...] + y
    mu = jnp.mean(h, axis=-1, keepdims=True)
    hc = h - mu
    var = jnp.mean(hc * hc, axis=-1, keepdims=True)
    o_ref[...] = hc * lax.rsqrt(var + LN_EPS) * lg_ref[...] + lb_ref[...]


def _out_proj_ln(mix_a, mix_b, x2, w_out, ln_g, ln_b, alpha, tm):
    m, d = x2.shape
    return pl.pallas_call(
        functools.partial(_out_kernel, alpha=alpha),
        out_shape=jax.ShapeDtypeStruct((m, d), F32),
        grid=(m // tm,),
        in_specs=[pl.BlockSpec((tm, WIDTH_A), lambda i: (i, 0)),
                  pl.BlockSpec((tm, WIDTH_B), lambda i: (i, 0)),
                  pl.BlockSpec((tm, d), lambda i: (i, 0)),
                  pl.BlockSpec((WIDTH_A + WIDTH_B, d), lambda i: (0, 0)),
                  pl.BlockSpec((1, d), lambda i: (0, 0)),
                  pl.BlockSpec((1, d), lambda i: (0, 0))],
        out_specs=pl.BlockSpec((tm, d), lambda i: (i, 0)),
        compiler_params=pltpu.CompilerParams(
            dimension_semantics=("arbitrary",), vmem_limit_bytes=VMEM_LIMIT_BYTES),
        name="out_proj_ln",
    )(mix_a, mix_b, x2, w_out, ln_g, ln_b)


def _pack_w_in(w):
    scale = LOG2E / math.sqrt(HEAD_DIM)
    sizes = (WIDTH_A,) * 4 + (WIDTH_B,) * 4 + (IDX_HEADS * IDX_DIM, IDX_DIM, IDX_HEADS)
    qa, ka, va, ga, qb, kb, vb, gb, qi, ki, wi = jnp.split(w, np.cumsum(sizes)[:-1].tolist(), axis=1)
    w_row = jnp.concatenate([qa * scale, ka, qb * scale, kb, qi, ki, ki, ga, gb], axis=1)
    wi_pad = jnp.pad(wi, ((0, 0), (0, T_WI_ROWS - IDX_HEADS)))
    w_t = jnp.concatenate([va, vb, wi_pad], axis=1).T
    return w_row.astype(BF16), w_t.astype(BF16)


def _layer(x, w_in, w_out, ln_g, ln_b, rel_bias, alpha, topk):
    b, s, d = x.shape
    tq = min(256, s)
    tm = min(512, b * s)
    x2 = x.reshape(b * s, d)
    w_row, w_t = _pack_w_in(w_in)
    p2, g2, vt, wi_t = _proj(x2, w_row, w_t, tm=tm, tk=tq)
    p3 = p2.reshape(b, s, P_COLS)
    g3 = g2.reshape(b, s, G_COLS)
    mix_a = _sb_attention(p3, g3, vt, tq)
    mix_b = _dsa_attention(p3, g3, vt, wi_t, rel_bias, tq, topk)
    out = _out_proj_ln(mix_a.reshape(b * s, WIDTH_A), mix_b.reshape(b * s, WIDTH_B), x2,
                       w_out.astype(BF16), ln_g.reshape(1, d), ln_b.reshape(1, d), alpha, tm=tm)
    return out.reshape(b, s, d)


def kernel(x, w_in, w_out, ln_g, ln_b, rel_bias):
    depth = w_in.shape[0]
    alpha = (2.0 * depth) ** 0.25
    topk = min(TOPK_MAX, x.shape[1] // 4)
    h = x
    for layer in range(depth):
        h = _layer(h, w_in[layer], w_out[layer], ln_g[layer], ln_b[layer], rel_bias, alpha, topk)
    return h
```

```python
import functools
import math

import jax
import jax.numpy as jnp
import numpy as np
from jax import lax
from jax.experimental import pallas as pl
from jax.experimental.pallas import tpu as pltpu

HEAD_DIM = 64
HEADS_A = 8
HEADS_B = 8
WIDTH_A = HEADS_A * HEAD_DIM
WIDTH_B = HEADS_B * HEAD_DIM
IDX_HEADS = 8
IDX_DIM = 64
CHUNK = 64
TOPK_MAX = 256
N_BUCKETS = 32
MAX_DISTANCE = 128
LN_EPS = 1e-5

LANES = 128
SUBLANES = 8
HEADS_PER_TILE = LANES // HEAD_DIM
VMEM_LIMIT_BYTES = 56 * 1024 * 1024

P_QA, P_KA, P_QB, P_KB, P_QI = (i * 512 for i in range(5))
P_KI = 5 * 512
P_COLS = P_KI + LANES
G_GA, G_GB = 0, 512
G_COLS = 1024
T_VA, T_VB, T_WI = 0, 512, 1024
T_WI_ROWS = 16
T_ROWS = T_WI + T_WI_ROWS

NEG_MASK = -1e30
MOST_NEGATIVE = float(np.finfo(np.float32).min)
NEG_INF_KEY16 = -32641
NO_CUT = 2 ** 30
HALF16 = 2 ** 15
SOFTMAX_SLACK = 40.0
LOG2E = math.log2(math.e)
DENOM_ROWS = 16
STREAM_HEADS = 2
P1_GROUP = 4
FAR_GROUP = 4
SB_HEADS_PER_STEP = 8
LOG2_DEAD = -151.0

BF16 = jnp.bfloat16
F32 = jnp.float32


def _dot_nt(a, b):
    return lax.dot_general(a, b, (((1,), (1,)), ((), ())), preferred_element_type=F32)


def _dot(a, b):
    return jnp.dot(a, b, preferred_element_type=F32)


def _mask_head(x, parity):
    lane = lax.broadcasted_iota(jnp.int32, x.shape, 1)
    keep = (lane < HEAD_DIM) if parity == 0 else (lane >= HEAD_DIM)
    return jnp.where(keep, x, jnp.zeros_like(x))


def _tree_sum(xs):
    while len(xs) > 1:
        xs = [a + b for a, b in zip(xs[::2], xs[1::2])] + ([xs[-1]] if len(xs) % 2 else [])
    return xs[0]


def _silu(g):
    return g / (1.0 + jnp.exp(-g))


def _proj_kernel(x_ref, wr_ref, wt_ref, p_ref, g_ref, vt_ref, wi_ref, *, tk):
    xb = x_ref[...].astype(BF16)
    for c in range(0, P_COLS, 512):
        cw = min(512, P_COLS - c)
        p_ref[:, c:c + cw] = _dot(xb, wr_ref[:, c:c + cw]).astype(BF16)
    for c in range(0, G_COLS, 512):
        g_ref[:, c:c + 512] = _dot(xb, wr_ref[:, P_COLS + c:P_COLS + c + 512])
    for r in range(0, T_WI, 512):
        t = _dot_nt(wt_ref[r:r + 512, :], xb).astype(BF16)
        for kb in range(xb.shape[0] // tk):
            vt_ref[kb, r:r + 512, :] = t[:, kb * tk:(kb + 1) * tk]
    wi_ref[...] = _dot_nt(wt_ref[T_WI:, :], xb)


def _proj(x2, w_row, w_t, tm, tk):
    m, d = x2.shape
    return pl.pallas_call(
        functools.partial(_proj_kernel, tk=tk),
        out_shape=(jax.ShapeDtypeStruct((m, P_COLS), BF16),
                   jax.ShapeDtypeStruct((m, G_COLS), F32),
                   jax.ShapeDtypeStruct((m // tk, T_WI, tk), BF16),
                   jax.ShapeDtypeStruct((T_WI_ROWS, m), F32)),
        grid=(m // tm,),
        in_specs=[pl.BlockSpec((tm, d), lambda i: (i, 0)),
                  pl.BlockSpec((d, P_COLS + G_COLS), lambda i: (0, 0)),
                  pl.BlockSpec((T_ROWS, d), lambda i: (0, 0))],
        out_specs=(pl.BlockSpec((tm, P_COLS), lambda i: (i, 0)),
                   pl.BlockSpec((tm, G_COLS), lambda i: (i, 0)),
                   pl.BlockSpec((tm // tk, T_WI, tk), lambda i: (i, 0, 0)),
                   pl.BlockSpec((T_WI_ROWS, tm), lambda i: (0, i))),
        compiler_params=pltpu.CompilerParams(
            dimension_semantics=("arbitrary",), vmem_limit_bytes=VMEM_LIMIT_BYTES),
        name="proj",
    )(x2, w_row, w_t)


def _softplus2(z):
    return jnp.maximum(z, 0.0) + jnp.log2(1.0 + jnp.exp2(-jnp.abs(z)))


def _sb_blocks(chains, u2):
    half = u2.shape[0]
    stricts = [c[3] for c in chains]
    masked = lambda xs: [x if m is None else jnp.where(m, x, 0.0) for x, m in zip(xs, stricts)]
    zts = [_dot_nt(kblk, qh) for qh, kblk, _, _ in chains]
    sps = masked([_softplus2(zt) for zt in zts])
    his = [sp.astype(BF16) for sp in sps]
    los = [(sp - hi.astype(F32)).astype(BF16) for sp, hi in zip(sps, his)]
    halves = lambda x: (x[:half], x[half:])
    tails = []
    for sp, hi, lo in zip(sps, his, los):
        (hi_a, hi_b), (lo_a, lo_b) = halves(hi), halves(lo)
        tail_a = _dot(u2, jnp.concatenate([hi_a, lo_a], axis=0))
        tail_b = _dot(u2, jnp.concatenate([hi_b, lo_b], axis=0))
        total_b = tail_b[:1, :] - sp[half:half + 1, :]
        tails.append(jnp.concatenate([tail_a + total_b, tail_b], axis=0))
    ws = masked([jnp.exp2(zt - sp + tail) for zt, sp, tail in zip(zts, sps, tails)])
    pvs = [_dot(c[2], a.astype(BF16)) for c, a in zip(chains, ws)]
    totals = [tail[:1, :] - sp[:1, :] for tail, sp in zip(tails, sps)]
    return list(zip(pvs, totals))


def _sb_kernel(q_ref, k_ref, vt_ref, u_ref, g_ref, o_ref, *, tq, n_heads):
    qi = pl.program_id(2)
    pair = lambda h: slice((h // HEADS_PER_TILE) * LANES, (h // HEADS_PER_TILE + 1) * LANES)
    qh = [_mask_head(q_ref[:, pair(h)], h % HEADS_PER_TILE) for h in range(n_heads)]
    u2 = u_ref[...]
    key_i = lax.broadcasted_iota(jnp.int32, (tq, tq), 0)
    qry_i = lax.broadcasted_iota(jnp.int32, (tq, tq), 1)
    strict = key_i < qry_i

    def chains(j, mask):
        start = pl.multiple_of(j * tq, tq)
        return [(qh[h], k_ref[pl.ds(start, tq), pair(h)], vt_ref[j, pair(h), :], mask)
                for h in range(n_heads)]

    has_prev = qi > 0
    res = _sb_blocks(chains(qi, strict) + chains(jnp.maximum(qi - 1, 0), None), u2)
    state = []
    for h in range(n_heads):
        (pv_d, tot_d), (pv_p, tot_p) = res[h], res[n_heads + h]
        state += [pv_d + jnp.where(has_prev, jnp.exp2(tot_d), 0.0) * pv_p, tot_d + tot_p]

    def alive_after(st):
        top = functools.reduce(jnp.maximum, st[1::2])
        return jnp.max(top) > LOG2_DEAD

    def cond(carry):
        j, alive = carry[0], carry[1]
        return jnp.logical_and(j >= 0, alive)

    def body(carry):
        j, st = carry[0], carry[2:]
        res = _sb_blocks(chains(j, None), u2)
        out = []
        for h in range(n_heads):
            acc, run = st[2 * h], st[2 * h + 1]
            pv, total = res[h]
            out += [acc + jnp.exp2(run) * pv, run + total]
        return (j - 1, alive_after(out)) + tuple(out)

    fin = lax.while_loop(cond, body, (qi - 2, alive_after(state)) + tuple(state))[2:]
    row = lax.broadcasted_iota(jnp.int32, fin[0].shape, 0)
    for hp in range(n_heads // HEADS_PER_TILE):
        acc0, acc1 = fin[2 * (2 * hp)], fin[2 * (2 * hp + 1)]
        ot = jnp.where(row < HEAD_DIM, acc0, acc1)
        o_ref[:, pair(2 * hp)] = (ot.T * _silu(g_ref[:, pair(2 * hp)])).astype(o_ref.dtype)


def _sb_attention(p3, g3, vt, tq):
    b, s, _ = p3.shape
    nkb = s // tq
    half = tq // 2
    tri = np.triu(-np.ones((half, half), np.float32), 1)
    u2 = jnp.asarray(np.concatenate([tri, tri], axis=1), BF16)
    w = SB_HEADS_PER_STEP * HEAD_DIM
    return pl.pallas_call(
        functools.partial(_sb_kernel, tq=tq, n_heads=SB_HEADS_PER_STEP),
        out_shape=jax.ShapeDtypeStruct((b, s, WIDTH_A), BF16),
        grid=(b, HEADS_A // SB_HEADS_PER_STEP, nkb),
        in_specs=[
            pl.BlockSpec((None, tq, w), lambda bi, hp, qi: (bi, qi, P_QA // w + hp)),
            pl.BlockSpec((None, s, w), lambda bi, hp, qi: (bi, 0, P_KA // w + hp)),
            pl.BlockSpec((nkb, w, tq), lambda bi, hp, qi: (bi, T_VA // w + hp, 0)),
            pl.BlockSpec((half, 2 * half), lambda bi, hp, qi: (0, 0)),
            pl.BlockSpec((None, tq, w), lambda bi, hp, qi: (bi, qi, G_GA // w + hp)),
        ],
        out_specs=pl.BlockSpec((None, tq, w), lambda bi, hp, qi: (bi, qi, hp)),
        compiler_params=pltpu.CompilerParams(
            dimension_semantics=("arbitrary", "arbitrary", "arbitrary"),
            vmem_limit_bytes=VMEM_LIMIT_BYTES),
        name="sb_attention",
    )(p3, p3, vt, u2, g3)


def _dsa_kernel(rb_ref, far_ref, qi_ref, wi_ref, ki_ref, q_ref, k_ref, vt_ref, bkt_ref, g_ref,
                o_ref, sc_ref, hi_ref, bias_ref, m_ref, acc_ref, *, tq, topk, seq):
    qi = pl.program_id(1)
    nblk = qi + 1

    @pl.when(jnp.logical_and(pl.program_id(0) == 0, qi == 0))
    def _():
        far = far_ref[0]
        for h in range(HEADS_B):
            for d in range(2):
                bkt = bkt_ref[d]

                def bucket_body(b, tile, bkt=bkt, h=h):
                    return jnp.where(bkt == b, (rb_ref[b, h] - rb_ref[far, h]) * LOG2E, tile)

                bias_ref[h, d] = lax.fori_loop(0, N_BUCKETS, bucket_body, jnp.zeros((tq, tq), F32))

    wi = wi_ref[...]
    qih = [_mask_head(qi_ref[:, (h // 2) * LANES:(h // 2 + 1) * LANES], h % 2)
           for h in range(IDX_HEADS)]

    def score_block(j, nb=1):
        start = pl.multiple_of(j * tq, tq)
        kib = ki_ref[pl.ds(start, tq * nb), :]
        rels = [_dot_nt(kib, qih[h]) for h in range(IDX_HEADS)]
        sc = jnp.zeros((tq * nb, tq), F32)
        for h in range(IDX_HEADS):
            sc = sc + jnp.maximum(rels[h], 0.0) * wi[h:h + 1, :]
        return sc

    def store_scores(j, sc):
        sc_ref[j] = sc
        hi_ref[j] = sc.astype(BF16)

    def score_group(j, nb):
        sc = score_block(j, nb)
        for i in range(nb):
            store_scores(j + i, sc[i * tq:(i + 1) * tq])

    def p1_body(i, c):
        score_group(P1_GROUP * i, P1_GROUP)
        return c

    n_groups1 = qi // P1_GROUP
    lax.fori_loop(0, n_groups1, p1_body, 0)
    done1 = n_groups1 * P1_GROUP
    size = P1_GROUP // 2
    while size >= 1:
        take = (qi // size) % 2 == 1

        @pl.when(take)
        def _(done1=done1, size=size):
            score_group(done1, size)

        done1 = done1 + jnp.where(take, size, 0)
        size //= 2

    key_i = lax.broadcasted_iota(jnp.int32, (tq, tq), 0)
    qry_i = lax.broadcasted_iota(jnp.int32, (tq, tq), 1)
    admissible = key_i < (qry_i // CHUNK + 1) * CHUNK
    store_scores(qi, jnp.where(admissible, score_block(qi), -jnp.inf))

    def count(pred):
        def body(i, acc):
            for j in (2 * i, 2 * i + 1):
                ones = jnp.where(pred(sc_ref[j], j * tq), 1.0, 0.0)
                acc = acc + _tree_sum([ones[r:r + SUBLANES, :] for r in range(0, tq, SUBLANES)])
            return acc
        acc = lax.fori_loop(0, (nblk + 1) // 2, body, jnp.zeros((SUBLANES, tq), F32))
        return jnp.sum(acc, axis=0, keepdims=True)

    def count_hi(cand):
        rows = 2 * SUBLANES
        def body(i, acc):
            for blk in (hi_ref[2 * i], hi_ref[2 * i + 1]):
                ones = jnp.where(blk >= cand, jnp.int16(1), jnp.int16(0))
                acc = acc + _tree_sum([ones[r:r + rows, :] for r in range(0, tq, rows)])
            return acc
        acc = lax.fori_loop(0, (nblk + 1) // 2, body, jnp.zeros((rows, tq), jnp.int16))
        return jnp.sum(acc.astype(F32), axis=0, keepdims=True)

    @pl.when(nblk % 2 == 1)
    def _():
        sc_ref[nblk] = jnp.full((tq, tq), -jnp.inf, F32)
        hi_ref[nblk] = jnp.full((tq, tq), -jnp.inf, BF16)

    def float_of_key(key):
        return lax.bitcast_convert_type(jnp.where(key < 0, key ^ 0x7FFFFFFF, key), F32)

    def key32_of_key16(key16):
        return key16 * (2 * HALF16) + jnp.where(key16 < 0, 2 * HALF16 - 1, 0)

    scanned = (nblk * tq).astype(F32)
    kf = float(topk)
    assert seq // (2 * SUBLANES) < 2 ** 15

    def hi_body(i, prefix):
        cand_u = prefix | jnp.left_shift(jnp.int32(1), 15 - i)
        key16 = cand_u - HALF16
        cand = float_of_key(key32_of_key16(key16)).astype(BF16)
        cnt = jnp.where(key16 <= NEG_INF_KEY16, scanned, count_hi(cand))
        return jnp.where(cnt >= kf, cand_u, prefix)

    hi_key = lax.fori_loop(0, 16, hi_body, jnp.zeros((1, tq), jnp.int32)) - HALF16
    finite_thr = hi_key > NEG_INF_KEY16
    base = jnp.where(finite_thr, key32_of_key16(hi_key) - HALF16, 0)
    base_val = float_of_key(base)
    base_cnt = count(lambda blk, k0: blk >= base_val)

    def lo_body(i, carry):
        off, cnt_ge = carry
        off_try = off | jnp.left_shift(jnp.int32(1), 16 - i)
        cand = float_of_key(base + off_try)
        cnt = count(lambda blk, k0: blk >= cand)
        take = cnt >= kf
        return jnp.where(take, off_try, off), jnp.where(take, cnt, cnt_ge)

    off, cnt_ge = lax.fori_loop(0, 17, lo_body, (jnp.zeros((1, tq), jnp.int32), base_cnt))
    thr = float_of_key(base + off)

    tie = jnp.logical_and(cnt_ge > kf, finite_thr)
    has_tie = jnp.max(jnp.where(tie, 1.0, 0.0)) > 0.0

    @pl.when(jnp.logical_not(has_tie))
    def _():
        thr_min = jnp.where(finite_thr, thr, MOST_NEGATIVE)

        def mask_body(j, c):
            sc_ref[j] = jnp.where(sc_ref[j] >= thr_min, 0.0, NEG_MASK)
            return c

        lax.fori_loop(0, nblk, mask_body, 0)

    @pl.when(has_tie)
    def _():
        cnt_gt = count(lambda blk, k0: blk > thr)
        need = kf - cnt_gt
        nbits = int(seq).bit_length()

        def idx_body(i, lo):
            c_try = lo + jnp.left_shift(jnp.int32(1), nbits - 1 - i)
            f = count(lambda blk, k0: jnp.logical_and(blk == thr, (key_i + k0) < c_try))
            return jnp.where(f < need, c_try, lo)

        lo = lax.fori_loop(0, nbits, idx_body, jnp.zeros((1, tq), jnp.int32))
        cut = jnp.where(tie, lo + 1, jnp.where(finite_thr, NO_CUT, 0))
        thr_gt = jnp.where(finite_thr, thr, -jnp.inf)

        def mask_body(j, c):
            blk = sc_ref[j]
            at_thr = jnp.where((key_i + j * tq) < cut, 0.0, NEG_MASK)
            sc_ref[j] = jnp.where(blk > thr_gt, 0.0, jnp.where(blk == thr_gt, at_thr, NEG_MASK))
            return c

        lax.fori_loop(0, nblk, mask_body, 0)

    qh = [_mask_head(q_ref[:, (h // 2) * LANES:(h // 2 + 1) * LANES], h % 2)
          for h in range(HEADS_B)]
    m_ref[...] = jnp.full(m_ref.shape, NEG_MASK, F32)
    acc_ref[...] = jnp.zeros(acc_ref.shape, F32)

    def att_block(j, nb, d, stream=False):
        start = pl.multiple_of(j * tq, tq)
        cat = lambda xs, axis: xs[0] if nb == 1 else jnp.concatenate(xs, axis=axis)
        mask = cat([sc_ref[j + i] for i in range(nb)], 0)
        ones = jnp.ones((DENOM_ROWS, tq * nb), BF16)
        vts = [jnp.concatenate(
            [cat([vt_ref[j + i, h * HEAD_DIM:(h + 1) * HEAD_DIM, :] for i in range(nb)], 1), ones], axis=0)
            for h in range(HEADS_B)]
        pair = lambda h: slice((h // HEADS_PER_TILE) * LANES, (h // HEADS_PER_TILE + 1) * LANES)
        qk = lambda h: _dot_nt(k_ref[pl.ds(start, tq * nb), pair(h)], qh[h])
        m_old = m_ref[...]
        hs = range(HEADS_B)

        def biased_logits(group):
            lgs = [qk(h) + mask for h in group]
            if d is not None:
                lgs = [lg + bias_ref[h, d] for lg, h in zip(lgs, group)]
            return lgs

        if stream:
            bmax, pvs = [], []
            for h0 in range(0, HEADS_B, STREAM_HEADS):
                group = range(h0, h0 + STREAM_HEADS)
                lgs = biased_logits(group)
                ps = [jnp.exp2(lg - m_old[h:h + 1, :]).astype(BF16) for lg, h in zip(lgs, group)]
                bmax += [jnp.max(lg, axis=0, keepdims=True) for lg in lgs]
                pvs += [_dot(vts[h], p) for p, h in zip(ps, group)]
            ok = jnp.max(jnp.concatenate(bmax, axis=0) - m_old) <= SOFTMAX_SLACK

            @pl.when(ok)
            def _():
                for h in hs:
                    acc_ref[h] = acc_ref[h] + pvs[h]

            return ok
        logits = biased_logits(hs)
        m_new = [jnp.maximum(m_old[h:h + 1, :], jnp.max(logits[h], axis=0, keepdims=True)) for h in hs]
        ps = [jnp.exp2(logits[h] - m_new[h]).astype(BF16) for h in hs]
        pvs = [_dot(vts[h], ps[h]) for h in hs]
        accs = [jnp.exp2(m_old[h:h + 1, :] - m_new[h]) * acc_ref[h] + pvs[h] for h in hs]
        m_ref[...] = jnp.concatenate(m_new, axis=0)
        for h in range(HEADS_B):
            acc_ref[h] = accs[h]

    def single_pass_or_exact(j, nb, d):
        ok = att_block(j, nb, d, stream=True)

        @pl.when(jnp.logical_not(ok))
        def _():
            att_block(j, nb, d)

    att_block(qi, 1, 0)

    @pl.when(qi > 0)
    def _():
        single_pass_or_exact(qi - 1, 1, 1)

    n_far = qi - 1
    far_blocks = lambda j, nb: single_pass_or_exact(j, nb, None)

    def far_body(i, c):
        far_blocks(FAR_GROUP * i, FAR_GROUP)
        return c

    n_groups = jnp.maximum(n_far, 0) // FAR_GROUP
    lax.fori_loop(0, n_groups, far_body, 0)
    done = n_groups * FAR_GROUP
    size = FAR_GROUP // 2
    while size >= 1:
        take = jnp.logical_and(n_far > 0, (n_far // size) % 2 == 1)

        @pl.when(take)
        def _(done=done, size=size):
            far_blocks(done, size)

        done = done + jnp.where(take, size, 0)
        size //= 2

    for hp in range(HEADS_B // HEADS_PER_TILE):
        h0 = hp * HEADS_PER_TILE
        a0, a1 = acc_ref[h0], acc_ref[h0 + 1]
        ot = jnp.concatenate([a0[:HEAD_DIM] / a0[HEAD_DIM:HEAD_DIM + 1, :],
                              a1[:HEAD_DIM] / a1[HEAD_DIM:HEAD_DIM + 1, :]], axis=0)
        gp = g_ref[:, hp * LANES:(hp + 1) * LANES]
        o_ref[:, hp * LANES:(hp + 1) * LANES] = (ot.T * _silu(gp)).astype(o_ref.dtype)


def _t5_bucket(rel):
    half = N_BUCKETS // 2
    max_exact = half // 2
    ret = jnp.where(rel > 0, half, 0)
    n = jnp.abs(rel)
    nf = jnp.maximum(n, 1).astype(jnp.float32)
    large = max_exact + (jnp.log(nf / max_exact) / math.log(MAX_DISTANCE / max_exact)
                         * (half - max_exact)).astype(jnp.int32)
    large = jnp.minimum(large, half - 1)
    return ret + jnp.where(n < max_exact, n, large)


def _dsa_attention(p3, g3, vt, wi_t, rel_bias, tq, topk):
    b, s, _ = p3.shape
    nkb = s // tq
    assert tq >= MAX_DISTANCE
    key = jnp.arange(tq, dtype=jnp.int32)[:, None]
    qry = jnp.arange(tq, dtype=jnp.int32)[None, :]
    buckets = jnp.stack([_t5_bucket(key - qry), _t5_bucket(key - qry - tq)])
    far = _t5_bucket(jnp.full((1,), -tq - 1, jnp.int32))
    buckets = jnp.bitwise_and(buckets, 2 * N_BUCKETS - 1)
    far = jnp.bitwise_and(far, 2 * N_BUCKETS - 1)
    row = lambda blk: (lambda bi, qi: (bi, qi, blk))
    full = lambda blk: (lambda bi, qi: (bi, 0, blk))
    smem = pl.BlockSpec(memory_space=pltpu.SMEM)
    return pl.pallas_call(
        functools.partial(_dsa_kernel, tq=tq, topk=topk, seq=s),
        out_shape=jax.ShapeDtypeStruct((b, s, WIDTH_B), BF16),
        grid=(b, nkb),
        in_specs=[
            smem, smem,
            pl.BlockSpec((None, tq, 512), row(P_QI // 512)),
            pl.BlockSpec((T_WI_ROWS, tq), lambda bi, qi: (0, bi * nkb + qi)),
            pl.BlockSpec((None, s, LANES), full(P_KI // LANES)),
            pl.BlockSpec((None, tq, 512), row(P_QB // 512)),
            pl.BlockSpec((None, s, 512), full(P_KB // 512)),
            pl.BlockSpec((nkb, WIDTH_B, tq), lambda bi, qi: (bi, T_VB // WIDTH_B, 0)),
            pl.BlockSpec((2, tq, tq), lambda bi, qi: (0, 0, 0)),
            pl.BlockSpec((None, tq, 512), row(G_GB // 512)),
        ],
        out_specs=pl.BlockSpec((None, tq, WIDTH_B), lambda bi, qi: (bi, qi, 0)),
        scratch_shapes=[pltpu.VMEM((nkb + nkb % 2, tq, tq), F32),
                        pltpu.VMEM((nkb + nkb % 2, tq, tq), BF16),
                        pltpu.VMEM((HEADS_B, 2, tq, tq), F32),
                        pltpu.VMEM((HEADS_B, tq), F32),
                        pltpu.VMEM((HEADS_B, HEAD_DIM + DENOM_ROWS, tq), F32)],
        compiler_params=pltpu.CompilerParams(
            dimension_semantics=("arbitrary", "arbitrary"),
            vmem_limit_bytes=VMEM_LIMIT_BYTES),
        name="dsa_attention",
    )(rel_bias, far, p3, wi_t, p3, p3, p3, vt, buckets, g3)


def _out_kernel(ma_ref, mb_ref, x_ref, w_ref, lg_ref, lb_ref, o_ref, *, alpha):
    y = _dot(ma_ref[...], w_ref[:WIDTH_A, :]) + _dot(mb_ref[...], w_ref[WIDTH_A:, :])
    h = alpha * x_ref[...] + y
    mu = jnp.mean(h, axis=-1, keepdims=True)
    hc = h - mu
    var = jnp.mean(hc * hc, axis=-1, keepdims=True)
    o_ref[...] = hc * lax.rsqrt(var + LN_EPS) * lg_ref[...] + lb_ref[...]


def _out_proj_ln(mix_a, mix_b, x2, w_out, ln_g, ln_b, alpha, tm):
    m, d = x2.shape
    return pl.pallas_call(
        functools.partial(_out_kernel, alpha=alpha),
        out_shape=jax.ShapeDtypeStruct((m, d), F32),
        grid=(m // tm,),
        in_specs=[pl.BlockSpec((tm, WIDTH_A), lambda i: (i, 0)),
                  pl.BlockSpec((tm, WIDTH_B), lambda i: (i, 0)),
                  pl.BlockSpec((tm, d), lambda i: (i, 0)),
                  pl.BlockSpec((WIDTH_A + WIDTH_B, d), lambda i: (0, 0)),
                  pl.BlockSpec((1, d), lambda i: (0, 0)),
                  pl.BlockSpec((1, d), lambda i: (0, 0))],
        out_specs=pl.BlockSpec((tm, d), lambda i: (i, 0)),
        compiler_params=pltpu.CompilerParams(
            dimension_semantics=("arbitrary",), vmem_limit_bytes=VMEM_LIMIT_BYTES),
        name="out_proj_ln",
    )(mix_a, mix_b, x2, w_out, ln_g, ln_b)


def _pack_w_in(w):
    scale = LOG2E / math.sqrt(HEAD_DIM)
    sizes = (WIDTH_A,) * 4 + (WIDTH_B,) * 4 + (IDX_HEADS * IDX_DIM, IDX_DIM, IDX_HEADS)
    qa, ka, va, ga, qb, kb, vb, gb, qi, ki, wi = jnp.split(w, np.cumsum(sizes)[:-1].tolist(), axis=1)
    w_row = jnp.concatenate([qa * scale, ka, qb * scale, kb, qi, ki, ki, ga, gb], axis=1)
    wi_pad = jnp.pad(wi, ((0, 0), (0, T_WI_ROWS - IDX_HEADS)))
    w_t = jnp.concatenate([va, vb, wi_pad], axis=1).T
    return w_row.astype(BF16), w_t.astype(BF16)


def _layer(x, w_in, w_out, ln_g, ln_b, rel_bias, alpha, topk):
    b, s, d = x.shape
    tq = min(256, s)
    tm = min(512, b * s)
    x2 = x.reshape(b * s, d)
    w_row, w_t = _pack_w_in(w_in)
    p2, g2, vt, wi_t = _proj(x2, w_row, w_t, tm=tm, tk=tq)
    p3 = p2.reshape(b, s, P_COLS)
    g3 = g2.reshape(b, s, G_COLS)
    mix_a = _sb_attention(p3, g3, vt, tq)
    mix_b = _dsa_attention(p3, g3, vt, wi_t, rel_bias, tq, topk)
    out = _out_proj_ln(mix_a.reshape(b * s, WIDTH_A), mix_b.reshape(b * s, WIDTH_B), x2,
                       w_out.astype(BF16), ln_g.reshape(1, d), ln_b.reshape(1, d), alpha, tm=tm)
    return out.reshape(b, s, d)


def kernel(x, w_in, w_out, ln_g, ln_b, rel_bias):
    depth = w_in.shape[0]
    alpha = (2.0 * depth) ** 0.25
    topk = min(TOPK_MAX, x.shape[1] // 4)
    h = x
    for layer in range(depth):
        h = _layer(h, w_in[layer], w_out[layer], ln_g[layer], ln_b[layer], rel_bias, alpha, topk)
    return h
```

```python
import functools
import math

import jax
import jax.numpy as jnp
import numpy as np
from jax import lax
from jax.experimental import pallas as pl
from jax.experimental.pallas import tpu as pltpu

HEAD_DIM = 64
HEADS_A = 8
HEADS_B = 8
WIDTH_A = HEADS_A * HEAD_DIM
WIDTH_B = HEADS_B * HEAD_DIM
IDX_HEADS = 8
IDX_DIM = 64
CHUNK = 64
TOPK_MAX = 256
N_BUCKETS = 32
MAX_DISTANCE = 128
LN_EPS = 1e-5

LANES = 128
SUBLANES = 8
HEADS_PER_TILE = LANES // HEAD_DIM
VMEM_LIMIT_BYTES = 56 * 1024 * 1024

P_QA, P_KA, P_QB, P_KB, P_QI = (i * 512 for i in range(5))
P_KI = 5 * 512
P_COLS = P_KI + LANES
G_GA, G_GB = 0, 512
G_COLS = 1024
T_VA, T_VB, T_WI = 0, 512, 1024
T_WI_ROWS = 16
T_ROWS = T_WI + T_WI_ROWS

NEG_MASK = -1e30
MOST_NEGATIVE = float(np.finfo(np.float32).min)
NEG_INF_KEY16 = -32641
NO_CUT = 2 ** 30
NO_INDEX = 2 ** 15 - 1
HALF16 = 2 ** 15
SOFTMAX_SLACK = 40.0
LOG2E = math.log2(math.e)
DENOM_ROWS = 16
STREAM_HEADS = 8
P1_GROUP = 4
FAR_GROUP = 4
SB_HEADS_PER_STEP = 8
LOG2_DEAD = -151.0

BF16 = jnp.bfloat16
F32 = jnp.float32


def _dot_nt(a, b):
    return lax.dot_general(a, b, (((1,), (1,)), ((), ())), preferred_element_type=F32)


def _dot(a, b):
    return jnp.dot(a, b, preferred_element_type=F32)


def _mask_head(x, parity):
    lane = lax.broadcasted_iota(jnp.int32, x.shape, 1)
    keep = (lane < HEAD_DIM) if parity == 0 else (lane >= HEAD_DIM)
    return jnp.where(keep, x, jnp.zeros_like(x))


def _tree_sum(xs):
    while len(xs) > 1:
        xs = [a + b for a, b in zip(xs[::2], xs[1::2])] + ([xs[-1]] if len(xs) % 2 else [])
    return xs[0]


def _silu(g):
    return g / (1.0 + jnp.exp(-g))


def _proj_kernel(x_ref, wr_ref, wt_ref, p_ref, g_ref, vt_ref, wi_ref, *, tk):
    xb = x_ref[...].astype(BF16)
    for c in range(0, P_COLS, 512):
        cw = min(512, P_COLS - c)
        p_ref[:, c:c + cw] = _dot(xb, wr_ref[:, c:c + cw]).astype(BF16)
    for c in range(0, G_COLS, 512):
        g_ref[:, c:c + 512] = _dot(xb, wr_ref[:, P_COLS + c:P_COLS + c + 512])
    for r in range(0, T_WI, 512):
        t = _dot_nt(wt_ref[r:r + 512, :], xb).astype(BF16)
        for kb in range(xb.shape[0] // tk):
            vt_ref[kb, r:r + 512, :] = t[:, kb * tk:(kb + 1) * tk]
    wi_ref[...] = _dot_nt(wt_ref[T_WI:, :], xb)


def _proj(x2, w_row, w_t, tm, tk):
    m, d = x2.shape
    return pl.pallas_call(
        functools.partial(_proj_kernel, tk=tk),
        out_shape=(jax.ShapeDtypeStruct((m, P_COLS), BF16),
                   jax.ShapeDtypeStruct((m, G_COLS), F32),
                   jax.ShapeDtypeStruct((m // tk, T_WI, tk), BF16),
                   jax.ShapeDtypeStruct((T_WI_ROWS, m), F32)),
        grid=(m // tm,),
        in_specs=[pl.BlockSpec((tm, d), lambda i: (i, 0)),
                  pl.BlockSpec((d, P_COLS + G_COLS), lambda i: (0, 0)),
                  pl.BlockSpec((T_ROWS, d), lambda i: (0, 0))],
        out_specs=(pl.BlockSpec((tm, P_COLS), lambda i: (i, 0)),
                   pl.BlockSpec((tm, G_COLS), lambda i: (i, 0)),
                   pl.BlockSpec((tm // tk, T_WI, tk), lambda i: (i, 0, 0)),
                   pl.BlockSpec((T_WI_ROWS, tm), lambda i: (0, i))),
        compiler_params=pltpu.CompilerParams(
            dimension_semantics=("arbitrary",), vmem_limit_bytes=VMEM_LIMIT_BYTES),
        name="proj",
    )(x2, w_row, w_t)


def _softplus2(z):
    return jnp.maximum(z, 0.0) + jnp.log2(1.0 + jnp.exp2(-jnp.abs(z)))


def _sb_blocks(chains, u2):
    half = u2.shape[0]
    stricts = [c[3] for c in chains]
    masked = lambda xs: [x if m is None else jnp.where(m, x, 0.0) for x, m in zip(xs, stricts)]
    zts = [_dot_nt(kblk, qh) for qh, kblk, _, _ in chains]
    sps = masked([_softplus2(zt) for zt in zts])
    his = [sp.astype(BF16) for sp in sps]
    los = [(sp - hi.astype(F32)).astype(BF16) for sp, hi in zip(sps, his)]
    halves = lambda x: (x[:half], x[half:])
    tails = []
    for sp, hi, lo in zip(sps, his, los):
        (hi_a, hi_b), (lo_a, lo_b) = halves(hi), halves(lo)
        tail_a = _dot(u2, jnp.concatenate([hi_a, lo_a], axis=0))
        tail_b = _dot(u2, jnp.concatenate([hi_b, lo_b], axis=0))
        total_b = tail_b[:1, :] - sp[half:half + 1, :]
        tails.append(jnp.concatenate([tail_a + total_b, tail_b], axis=0))
    ws = masked([jnp.exp2(zt - sp + tail) for zt, sp, tail in zip(zts, sps, tails)])
    pvs = [_dot(c[2], a.astype(BF16)) for c, a in zip(chains, ws)]
    totals = [tail[:1, :] - sp[:1, :] for tail, sp in zip(tails, sps)]
    return list(zip(pvs, totals))


def _sb_kernel(q_ref, k_ref, vt_ref, u_ref, g_ref, o_ref, *, tq, n_heads):
    qi = pl.program_id(2)
    pair = lambda h: slice((h // HEADS_PER_TILE) * LANES, (h // HEADS_PER_TILE + 1) * LANES)
    qh = [_mask_head(q_ref[:, pair(h)], h % HEADS_PER_TILE) for h in range(n_heads)]
    u2 = u_ref[...]
    key_i = lax.broadcasted_iota(jnp.int32, (tq, tq), 0)
    qry_i = lax.broadcasted_iota(jnp.int32, (tq, tq), 1)
    strict = key_i < qry_i

    def chains(j, mask):
        start = pl.multiple_of(j * tq, tq)
        return [(qh[h], k_ref[pl.ds(start, tq), pair(h)], vt_ref[j, pair(h), :], mask)
                for h in range(n_heads)]

    has_prev = qi > 0
    res = _sb_blocks(chains(qi, strict) + chains(jnp.maximum(qi - 1, 0), None), u2)
    state = []
    for h in range(n_heads):
        (pv_d, tot_d), (pv_p, tot_p) = res[h], res[n_heads + h]
        state += [pv_d + jnp.where(has_prev, jnp.exp2(tot_d), 0.0) * pv_p, tot_d + tot_p]

    def alive_after(st):
        top = functools.reduce(jnp.maximum, st[1::2])
        return jnp.max(top) > LOG2_DEAD

    def cond(carry):
        j, alive = carry[0], carry[1]
        return jnp.logical_and(j >= 0, alive)

    def body(carry):
        j, st = carry[0], carry[2:]
        res = _sb_blocks(chains(j, None), u2)
        out = []
        for h in range(n_heads):
            acc, run = st[2 * h], st[2 * h + 1]
            pv, total = res[h]
            out += [acc + jnp.exp2(run) * pv, run + total]
        return (j - 1, alive_after(out)) + tuple(out)

    fin = lax.while_loop(cond, body, (qi - 2, alive_after(state)) + tuple(state))[2:]
    row = lax.broadcasted_iota(jnp.int32, fin[0].shape, 0)
    for hp in range(n_heads // HEADS_PER_TILE):
        acc0, acc1 = fin[2 * (2 * hp)], fin[2 * (2 * hp + 1)]
        ot = jnp.where(row < HEAD_DIM, acc0, acc1)
        o_ref[:, pair(2 * hp)] = (ot.T * _silu(g_ref[:, pair(2 * hp)])).astype(o_ref.dtype)


def _sb_attention(p3, g3, vt, tq):
    b, s, _ = p3.shape
    nkb = s // tq
    half = tq // 2
    tri = np.triu(-np.ones((half, half), np.float32), 1)
    u2 = jnp.asarray(np.concatenate([tri, tri], axis=1), BF16)
    w = SB_HEADS_PER_STEP * HEAD_DIM
    return pl.pallas_call(
        functools.partial(_sb_kernel, tq=tq, n_heads=SB_HEADS_PER_STEP),
        out_shape=jax.ShapeDtypeStruct((b, s, WIDTH_A), BF16),
        grid=(b, HEADS_A // SB_HEADS_PER_STEP, nkb),
        in_specs=[
            pl.BlockSpec((None, tq, w), lambda bi, hp, qi: (bi, qi, P_QA // w + hp)),
            pl.BlockSpec((None, s, w), lambda bi, hp, qi: (bi, 0, P_KA // w + hp)),
            pl.BlockSpec((nkb, w, tq), lambda bi, hp, qi: (bi, T_VA // w + hp, 0)),
            pl.BlockSpec((half, 2 * half), lambda bi, hp, qi: (0, 0)),
            pl.BlockSpec((None, tq, w), lambda bi, hp, qi: (bi, qi, G_GA // w + hp)),
        ],
        out_specs=pl.BlockSpec((None, tq, w), lambda bi, hp, qi: (bi, qi, hp)),
        compiler_params=pltpu.CompilerParams(
            dimension_semantics=("arbitrary", "arbitrary", "arbitrary"),
            vmem_limit_bytes=VMEM_LIMIT_BYTES),
        name="sb_attention",
    )(p3, p3, vt, u2, g3)


def _dsa_kernel(rb_ref, far_ref, qi_ref, wi_ref, ki_ref, q_ref, k_ref, vt_ref, bkt_ref, g_ref,
                o_ref, sc_ref, hi_ref, eq_ref, bias_ref, m_ref, acc_ref, *, tq, topk, seq):
    qi = pl.program_id(1)
    nblk = qi + 1

    @pl.when(jnp.logical_and(pl.program_id(0) == 0, qi == 0))
    def _():
        far = far_ref[0]
        for h in range(HEADS_B):
            for d in range(2):
                bkt = bkt_ref[d]

                def bucket_body(b, tile, bkt=bkt, h=h):
                    return jnp.where(bkt == b, (rb_ref[b, h] - rb_ref[far, h]) * LOG2E, tile)

                bias_ref[h, d] = lax.fori_loop(0, N_BUCKETS, bucket_body, jnp.zeros((tq, tq), F32))

    wi = wi_ref[...]
    qih = [_mask_head(qi_ref[:, (h // 2) * LANES:(h // 2 + 1) * LANES], h % 2)
           for h in range(IDX_HEADS)]

    def score_block(j, nb=1):
        start = pl.multiple_of(j * tq, tq)
        kib = ki_ref[pl.ds(start, tq * nb), :]
        rels = [_dot_nt(kib, qih[h]) for h in range(IDX_HEADS)]
        sc = jnp.zeros((tq * nb, tq), F32)
        for h in range(IDX_HEADS):
            sc = sc + jnp.maximum(rels[h], 0.0) * wi[h:h + 1, :]
        return sc

    def store_scores(j, sc):
        sc_ref[j] = sc
        hi_ref[j] = sc.astype(BF16)

    def score_group(j, nb):
        sc = score_block(j, nb)
        for i in range(nb):
            store_scores(j + i, sc[i * tq:(i + 1) * tq])

    def p1_body(i, c):
        score_group(P1_GROUP * i, P1_GROUP)
        return c

    n_groups1 = qi // P1_GROUP
    lax.fori_loop(0, n_groups1, p1_body, 0)
    done1 = n_groups1 * P1_GROUP
    size = P1_GROUP // 2
    while size >= 1:
        take = (qi // size) % 2 == 1

        @pl.when(take)
        def _(done1=done1, size=size):
            score_group(done1, size)

        done1 = done1 + jnp.where(take, size, 0)
        size //= 2

    key_i = lax.broadcasted_iota(jnp.int32, (tq, tq), 0)
    qry_i = lax.broadcasted_iota(jnp.int32, (tq, tq), 1)
    admissible = key_i < (qry_i // CHUNK + 1) * CHUNK
    store_scores(qi, jnp.where(admissible, score_block(qi), -jnp.inf))

    def count(pred):
        def body(i, acc):
            for j in (2 * i, 2 * i + 1):
                ones = jnp.where(pred(sc_ref[j], j * tq), 1.0, 0.0)
                acc = acc + _tree_sum([ones[r:r + SUBLANES, :] for r in range(0, tq, SUBLANES)])
            return acc
        acc = lax.fori_loop(0, (nblk + 1) // 2, body, jnp.zeros((SUBLANES, tq), F32))
        return jnp.sum(acc, axis=0, keepdims=True)

    def count16(ref, pred):
        rows = 2 * SUBLANES
        def body(i, acc):
            for blk in (ref[2 * i], ref[2 * i + 1]):
                ones = jnp.where(pred(blk), jnp.int16(1), jnp.int16(0))
                acc = acc + _tree_sum([ones[r:r + rows, :] for r in range(0, tq, rows)])
            return acc
        acc = lax.fori_loop(0, (nblk + 1) // 2, body, jnp.zeros((rows, tq), jnp.int16))
        return jnp.sum(acc.astype(F32), axis=0, keepdims=True)

    def count_hi(cand):
        return count16(hi_ref, lambda blk: blk >= cand)

    @pl.when(nblk % 2 == 1)
    def _():
        sc_ref[nblk] = jnp.full((tq, tq), -jnp.inf, F32)
        hi_ref[nblk] = jnp.full((tq, tq), -jnp.inf, BF16)

    def float_of_key(key):
        return lax.bitcast_convert_type(jnp.where(key < 0, key ^ 0x7FFFFFFF, key), F32)

    def key32_of_key16(key16):
        return key16 * (2 * HALF16) + jnp.where(key16 < 0, 2 * HALF16 - 1, 0)

    scanned = (nblk * tq).astype(F32)
    kf = float(topk)
    assert seq // (2 * SUBLANES) < 2 ** 15

    def hi_body(i, prefix):
        cand_u = prefix | jnp.left_shift(jnp.int32(1), 15 - i)
        key16 = cand_u - HALF16
        cand = float_of_key(key32_of_key16(key16)).astype(BF16)
        cnt = jnp.where(key16 <= NEG_INF_KEY16, scanned, count_hi(cand))
        return jnp.where(cnt >= kf, cand_u, prefix)

    hi_key = lax.fori_loop(0, 16, hi_body, jnp.zeros((1, tq), jnp.int32)) - HALF16
    finite_thr = hi_key > NEG_INF_KEY16
    base = jnp.where(finite_thr, key32_of_key16(hi_key) - HALF16, 0)
    base_val = float_of_key(base)
    base_cnt = count(lambda blk, k0: blk >= base_val)

    def lo_body(i, carry):
        off, cnt_ge = carry
        off_try = off | jnp.left_shift(jnp.int32(1), 16 - i)
        cand = float_of_key(base + off_try)
        cnt = count(lambda blk, k0: blk >= cand)
        take = cnt >= kf
        return jnp.where(take, off_try, off), jnp.where(take, cnt, cnt_ge)

    off, cnt_ge = lax.fori_loop(0, 17, lo_body, (jnp.zeros((1, tq), jnp.int32), base_cnt))
    thr = float_of_key(base + off)

    tie = jnp.logical_and(cnt_ge > kf, finite_thr)
    has_tie = jnp.max(jnp.where(tie, 1.0, 0.0)) > 0.0

    @pl.when(jnp.logical_not(has_tie))
    def _():
        thr_min = jnp.where(finite_thr, thr, MOST_NEGATIVE)

        def mask_body(j, c):
            sc_ref[j] = jnp.where(sc_ref[j] >= thr_min, 0.0, NEG_MASK)
            return c

        lax.fori_loop(0, nblk, mask_body, 0)

    @pl.when(has_tie)
    def _():
        cnt_gt = count(lambda blk, k0: blk > thr)
        need = kf - cnt_gt
        nbits = int(seq - 1).bit_length()
        assert seq < NO_INDEX

        def eq_body(j, c):
            idx = jnp.where(sc_ref[j] == thr, key_i + j * tq, NO_INDEX)
            eq_ref[j] = idx.astype(jnp.int16)
            return c

        lax.fori_loop(0, nblk, eq_body, 0)

        @pl.when(nblk % 2 == 1)
        def _():
            eq_ref[nblk] = jnp.full((tq, tq), NO_INDEX, jnp.int16)

        def idx_body(i, lo):
            c_try = lo + jnp.left_shift(jnp.int32(1), nbits - 1 - i)
            c16 = c_try.astype(jnp.int16)
            f = count16(eq_ref, lambda blk: blk < c16)
            return jnp.where(f < need, c_try, lo)

        lo = lax.fori_loop(0, nbits, idx_body, jnp.zeros((1, tq), jnp.int32))
        cut = jnp.where(tie, lo + 1, jnp.where(finite_thr, NO_CUT, 0))
        thr_gt = jnp.where(finite_thr, thr, -jnp.inf)

        def mask_body(j, c):
            blk = sc_ref[j]
            at_thr = jnp.where((key_i + j * tq) < cut, 0.0, NEG_MASK)
            sc_ref[j] = jnp.where(blk > thr_gt, 0.0, jnp.where(blk == thr_gt, at_thr, NEG_MASK))
            return c

        lax.fori_loop(0, nblk, mask_body, 0)

    qh = [_mask_head(q_ref[:, (h // 2) * LANES:(h // 2 + 1) * LANES], h % 2)
          for h in range(HEADS_B)]
    m_ref[...] = jnp.full(m_ref.shape, NEG_MASK, F32)
    acc_ref[...] = jnp.zeros(acc_ref.shape, F32)

    def att_block(j, nb, d, stream=False):
        start = pl.multiple_of(j * tq, tq)
        cat = lambda xs, axis: xs[0] if nb == 1 else jnp.concatenate(xs, axis=axis)
        mask = cat([sc_ref[j + i] for i in range(nb)], 0)
        ones = jnp.ones((DENOM_ROWS, tq * nb), BF16)
        vts = [jnp.concatenate(
            [cat([vt_ref[j + i, h * HEAD_DIM:(h + 1) * HEAD_DIM, :] for i in range(nb)], 1), ones], axis=0)
            for h in range(HEADS_B)]
        pair = lambda h: slice((h // HEADS_PER_TILE) * LANES, (h // HEADS_PER_TILE + 1) * LANES)
        qk = lambda h: _dot_nt(k_ref[pl.ds(start, tq * nb), pair(h)], qh[h])
        m_old = m_ref[...]
        hs = range(HEADS_B)

        def biased_logits(group):
            lgs = [qk(h) + mask for h in group]
            if d is not None:
                lgs = [lg + bias_ref[h, d] for lg, h in zip(lgs, group)]
            return lgs

        if stream:
            bmax, pvs = [], []
            for h0 in range(0, HEADS_B, STREAM_HEADS):
                group = range(h0, h0 + STREAM_HEADS)
                lgs = biased_logits(group)
                ps = [jnp.exp2(lg - m_old[h:h + 1, :]).astype(BF16) for lg, h in zip(lgs, group)]
                bmax += [jnp.max(lg, axis=0, keepdims=True) for lg in lgs]
                pvs += [_dot(vts[h], p) for p, h in zip(ps, group)]
            ok = jnp.max(jnp.concatenate(bmax, axis=0) - m_old) <= SOFTMAX_SLACK

            @pl.when(ok)
            def _():
                for h in hs:
                    acc_ref[h] = acc_ref[h] + pvs[h]

            return ok
        logits = biased_logits(hs)
        m_new = [jnp.maximum(m_old[h:h + 1, :], jnp.max(logits[h], axis=0, keepdims=True)) for h in hs]
        ps = [jnp.exp2(logits[h] - m_new[h]).astype(BF16) for h in hs]
        pvs = [_dot(vts[h], ps[h]) for h in hs]
        accs = [jnp.exp2(m_old[h:h + 1, :] - m_new[h]) * acc_ref[h] + pvs[h] for h in hs]
        m_ref[...] = jnp.concatenate(m_new, axis=0)
        for h in range(HEADS_B):
            acc_ref[h] = accs[h]

    def single_pass_or_exact(j, nb, d):
        ok = att_block(j, nb, d, stream=True)

        @pl.when(jnp.logical_not(ok))
        def _():
            att_block(j, nb, d)

    att_block(qi, 1, 0)

    @pl.when(qi > 0)
    def _():
        single_pass_or_exact(qi - 1, 1, 1)

    n_far = qi - 1
    far_blocks = lambda j, nb: single_pass_or_exact(j, nb, None)

    def far_body(i, c):
        far_blocks(FAR_GROUP * i, FAR_GROUP)
        return c

    n_groups = jnp.maximum(n_far, 0) // FAR_GROUP
    lax.fori_loop(0, n_groups, far_body, 0)
    done = n_groups * FAR_GROUP
    size = FAR_GROUP // 2
    while size >= 1:
        take = jnp.logical_and(n_far > 0, (n_far // size) % 2 == 1)

        @pl.when(take)
        def _(done=done, size=size):
            far_blocks(done, size)

        done = done + jnp.where(take, size, 0)
        size //= 2

    for hp in range(HEADS_B // HEADS_PER_TILE):
        h0 = hp * HEADS_PER_TILE
        a0, a1 = acc_ref[h0], acc_ref[h0 + 1]
        ot = jnp.concatenate([a0[:HEAD_DIM] / a0[HEAD_DIM:HEAD_DIM + 1, :],
                              a1[:HEAD_DIM] / a1[HEAD_DIM:HEAD_DIM + 1, :]], axis=0)
        gp = g_ref[:, hp * LANES:(hp + 1) * LANES]
        o_ref[:, hp * LANES:(hp + 1) * LANES] = (ot.T * _silu(gp)).astype(o_ref.dtype)


def _t5_bucket(rel):
    half = N_BUCKETS // 2
    max_exact = half // 2
    ret = jnp.where(rel > 0, half, 0)
    n = jnp.abs(rel)
    nf = jnp.maximum(n, 1).astype(jnp.float32)
    large = max_exact + (jnp.log(nf / max_exact) / math.log(MAX_DISTANCE / max_exact)
                         * (half - max_exact)).astype(jnp.int32)
    large = jnp.minimum(large, half - 1)
    return ret + jnp.where(n < max_exact, n, large)


def _dsa_attention(p3, g3, vt, wi_t, rel_bias, tq, topk):
    b, s, _ = p3.shape
    nkb = s // tq
    assert tq >= MAX_DISTANCE
    key = jnp.arange(tq, dtype=jnp.int32)[:, None]
    qry = jnp.arange(tq, dtype=jnp.int32)[None, :]
    buckets = jnp.stack([_t5_bucket(key - qry), _t5_bucket(key - qry - tq)])
    far = _t5_bucket(jnp.full((1,), -tq - 1, jnp.int32))
    buckets = jnp.bitwise_and(buckets, 2 * N_BUCKETS - 1)
    far = jnp.bitwise_and(far, 2 * N_BUCKETS - 1)
    row = lambda blk: (lambda bi, qi: (bi, qi, blk))
    full = lambda blk: (lambda bi, qi: (bi, 0, blk))
    smem = pl.BlockSpec(memory_space=pltpu.SMEM)
    return pl.pallas_call(
        functools.partial(_dsa_kernel, tq=tq, topk=topk, seq=s),
        out_shape=jax.ShapeDtypeStruct((b, s, WIDTH_B), BF16),
        grid=(b, nkb),
        in_specs=[
            smem, smem,
            pl.BlockSpec((None, tq, 512), row(P_QI // 512)),
            pl.BlockSpec((T_WI_ROWS, tq), lambda bi, qi: (0, bi * nkb + qi)),
            pl.BlockSpec((None, s, LANES), full(P_KI // LANES)),
            pl.BlockSpec((None, tq, 512), row(P_QB // 512)),
            pl.BlockSpec((None, s, 512), full(P_KB // 512)),
            pl.BlockSpec((nkb, WIDTH_B, tq), lambda bi, qi: (bi, T_VB // WIDTH_B, 0)),
            pl.BlockSpec((2, tq, tq), lambda bi, qi: (0, 0, 0)),
            pl.BlockSpec((None, tq, 512), row(G_GB // 512)),
        ],
        out_specs=pl.BlockSpec((None, tq, WIDTH_B), lambda bi, qi: (bi, qi, 0)),
        scratch_shapes=[pltpu.VMEM((nkb + nkb % 2, tq, tq), F32),
                        pltpu.VMEM((nkb + nkb % 2, tq, tq), BF16),
                        pltpu.VMEM((nkb + nkb % 2, tq, tq), jnp.int16),
                        pltpu.VMEM((HEADS_B, 2, tq, tq), F32),
                        pltpu.VMEM((HEADS_B, tq), F32),
                        pltpu.VMEM((HEADS_B, HEAD_DIM + DENOM_ROWS, tq), F32)],
        compiler_params=pltpu.CompilerParams(
            dimension_semantics=("arbitrary", "arbitrary"),
            vmem_limit_bytes=VMEM_LIMIT_BYTES),
        name="dsa_attention",
    )(rel_bias, far, p3, wi_t, p3, p3, p3, vt, buckets, g3)


def _out_kernel(ma_ref, mb_ref, x_ref, w_ref, lg_ref, lb_ref, o_ref, *, alpha):
    y = _dot(ma_ref[...], w_ref[:WIDTH_A, :]) + _dot(mb_ref[...], w_ref[WIDTH_A:, :])
    h = alpha * x_ref[...] + y
    mu = jnp.mean(h, axis=-1, keepdims=True)
    hc = h - mu
    var = jnp.mean(hc * hc, axis=-1, keepdims=True)
    o_ref[...] = hc * lax.rsqrt(var + LN_EPS) * lg_ref[...] + lb_ref[...]


def _out_proj_ln(mix_a, mix_b, x2, w_out, ln_g, ln_b, alpha, tm):
    m, d = x2.shape
    return pl.pallas_call(
        functools.partial(_out_kernel, alpha=alpha),
        out_shape=jax.ShapeDtypeStruct((m, d), F32),
        grid=(m // tm,),
        in_specs=[pl.BlockSpec((tm, WIDTH_A), lambda i: (i, 0)),
                  pl.BlockSpec((tm, WIDTH_B), lambda i: (i, 0)),
                  pl.BlockSpec((tm, d), lambda i: (i, 0)),
                  pl.BlockSpec((WIDTH_A + WIDTH_B, d), lambda i: (0, 0)),
                  pl.BlockSpec((1, d), lambda i: (0, 0)),
                  pl.BlockSpec((1, d), lambda i: (0, 0))],
        out_specs=pl.BlockSpec((tm, d), lambda i: (i, 0)),
        compiler_params=pltpu.CompilerParams(
            dimension_semantics=("arbitrary",), vmem_limit_bytes=VMEM_LIMIT_BYTES),
        name="out_proj_ln",
    )(mix_a, mix_b, x2, w_out, ln_g, ln_b)


def _pack_w_in(w):
    scale = LOG2E / math.sqrt(HEAD_DIM)
    sizes = (WIDTH_A,) * 4 + (WIDTH_B,) * 4 + (IDX_HEADS * IDX_DIM, IDX_DIM, IDX_HEADS)
    qa, ka, va, ga, qb, kb, vb, gb, qi, ki, wi = jnp.split(w, np.cumsum(sizes)[:-1].tolist(), axis=1)
    w_row = jnp.concatenate([qa * scale, ka, qb * scale, kb, qi, ki, ki, ga, gb], axis=1)
    wi_pad = jnp.pad(wi, ((0, 0), (0, T_WI_ROWS - IDX_HEADS)))
    w_t = jnp.concatenate([va, vb, wi_pad], axis=1).T
    return w_row.astype(BF16), w_t.astype(BF16)


def _layer(x, w_in, w_out, ln_g, ln_b, rel_bias, alpha, topk):
    b, s, d = x.shape
    tq = min(256, s)
    tm = min(512, b * s)
    x2 = x.reshape(b * s, d)
    w_row, w_t = _pack_w_in(w_in)
    p2, g2, vt, wi_t = _proj(x2, w_row, w_t, tm=tm, tk=tq)
    p3 = p2.reshape(b, s, P_COLS)
    g3 = g2.reshape(b, s, G_COLS)
    mix_a = _sb_attention(p3, g3, vt, tq)
    mix_b = _dsa_attention(p3, g3, vt, wi_t, rel_bias, tq, topk)
    out = _out_proj_ln(mix_a.reshape(b * s, WIDTH_A), mix_b.reshape(b * s, WIDTH_B), x2,
                       w_out.astype(BF16), ln_g.reshape(1, d), ln_b.reshape(1, d), alpha, tm=tm)
    return out.reshape(b, s, d)


def kernel(x, w_in, w_out, ln_g, ln_b, rel_bias):
    depth = w_in.shape[0]
    alpha = (2.0 * depth) ** 0.25
    topk = min(TOPK_MAX, x.shape[1] // 4)
    h = x
    for layer in range(depth):
        h = _layer(h, w_in[layer], w_out[layer], ln_g[layer], ln_b[layer], rel_bias, alpha, topk)
    return h
```

```python
import functools
import math

import jax
import jax.numpy as jnp
import numpy as np
from jax import lax
from jax.experimental import pallas as pl
from jax.experimental.pallas import tpu as pltpu

HEAD_DIM = 64
HEADS_A = 8
HEADS_B = 8
WIDTH_A = HEADS_A * HEAD_DIM
WIDTH_B = HEADS_B * HEAD_DIM
IDX_HEADS = 8
IDX_DIM = 64
CHUNK = 64
TOPK_MAX = 256
N_BUCKETS = 32
MAX_DISTANCE = 128
LN_EPS = 1e-5

LANES = 128
SUBLANES = 8
HEADS_PER_TILE = LANES // HEAD_DIM
VMEM_LIMIT_BYTES = 56 * 1024 * 1024

P_QA, P_KA, P_QB, P_KB, P_QI = (i * 512 for i in range(5))
P_KI = 5 * 512
P_COLS = P_KI + LANES
G_GA, G_GB = 0, 512
G_COLS = 1024
T_VA, T_VB, T_WI = 0, 512, 1024
T_WI_ROWS = 16
T_ROWS = T_WI + T_WI_ROWS

NEG_MASK = -1e30
MOST_NEGATIVE = float(np.finfo(np.float32).min)
NEG_INF_KEY16 = -32641
NO_CUT = 2 ** 30
NO_INDEX = 2 ** 15 - 1
HALF16 = 2 ** 15
SOFTMAX_SLACK = 40.0
LOG2E = math.log2(math.e)
DENOM_ROWS = 16
P1_GROUP = 4
FAR_GROUP = 4
SB_HEADS_PER_STEP = 8
LOG2_DEAD = -151.0

BF16 = jnp.bfloat16
F32 = jnp.float32


def _dot_nt(a, b):
    return lax.dot_general(a, b, (((1,), (1,)), ((), ())), preferred_element_type=F32)


def _dot(a, b):
    return jnp.dot(a, b, preferred_element_type=F32)


def _mask_head(x, parity):
    lane = lax.broadcasted_iota(jnp.int32, x.shape, 1)
    keep = (lane < HEAD_DIM) if parity == 0 else (lane >= HEAD_DIM)
    return jnp.where(keep, x, jnp.zeros_like(x))


def _tree_sum(xs):
    while len(xs) > 1:
        xs = [a + b for a, b in zip(xs[::2], xs[1::2])] + ([xs[-1]] if len(xs) % 2 else [])
    return xs[0]


def _silu(g):
    return g / (1.0 + jnp.exp(-g))


def _proj_kernel(x_ref, wr_ref, wt_ref, p_ref, g_ref, vt_ref, wi_ref, *, tk):
    xb = x_ref[...].astype(BF16)
    for c in range(0, P_COLS, 512):
        cw = min(512, P_COLS - c)
        p_ref[:, c:c + cw] = _dot(xb, wr_ref[:, c:c + cw]).astype(BF16)
    for c in range(0, G_COLS, 512):
        g_ref[:, c:c + 512] = _dot(xb, wr_ref[:, P_COLS + c:P_COLS + c + 512])
    for r in range(0, T_WI, 512):
        t = _dot_nt(wt_ref[r:r + 512, :], xb).astype(BF16)
        for kb in range(xb.shape[0] // tk):
            vt_ref[kb, r:r + 512, :] = t[:, kb * tk:(kb + 1) * tk]
    wi_ref[...] = _dot_nt(wt_ref[T_WI:, :], xb)


def _proj(x2, w_row, w_t, tm, tk):
    m, d = x2.shape
    return pl.pallas_call(
        functools.partial(_proj_kernel, tk=tk),
        out_shape=(jax.ShapeDtypeStruct((m, P_COLS), BF16),
                   jax.ShapeDtypeStruct((m, G_COLS), F32),
                   jax.ShapeDtypeStruct((m // tk, T_WI, tk), BF16),
                   jax.ShapeDtypeStruct((T_WI_ROWS, m), F32)),
        grid=(m // tm,),
        in_specs=[pl.BlockSpec((tm, d), lambda i: (i, 0)),
                  pl.BlockSpec((d, P_COLS + G_COLS), lambda i: (0, 0)),
                  pl.BlockSpec((T_ROWS, d), lambda i: (0, 0))],
        out_specs=(pl.BlockSpec((tm, P_COLS), lambda i: (i, 0)),
                   pl.BlockSpec((tm, G_COLS), lambda i: (i, 0)),
                   pl.BlockSpec((tm // tk, T_WI, tk), lambda i: (i, 0, 0)),
                   pl.BlockSpec((T_WI_ROWS, tm), lambda i: (0, i))),
        compiler_params=pltpu.CompilerParams(
            dimension_semantics=("arbitrary",), vmem_limit_bytes=VMEM_LIMIT_BYTES),
        name="proj",
    )(x2, w_row, w_t)


def _softplus2(z):
    return jnp.maximum(z, 0.0) + jnp.log2(1.0 + jnp.exp2(-jnp.abs(z)))


def _sb_blocks(chains, u2):
    half = u2.shape[0]
    stricts = [c[3] for c in chains]
    masked = lambda xs: [x if m is None else jnp.where(m, x, 0.0) for x, m in zip(xs, stricts)]
    zts = [_dot_nt(kblk, qh) for qh, kblk, _, _ in chains]
    sps = masked([_softplus2(zt) for zt in zts])
    his = [sp.astype(BF16) for sp in sps]
    los = [(sp - hi.astype(F32)).astype(BF16) for sp, hi in zip(sps, his)]
    halves = lambda x: (x[:half], x[half:])
    tails = []
    for sp, hi, lo in zip(sps, his, los):
        (hi_a, hi_b), (lo_a, lo_b) = halves(hi), halves(lo)
        tail_a = _dot(u2, jnp.concatenate([hi_a, lo_a], axis=0))
        tail_b = _dot(u2, jnp.concatenate([hi_b, lo_b], axis=0))
        total_b = tail_b[:1, :] - sp[half:half + 1, :]
        tails.append(jnp.concatenate([tail_a + total_b, tail_b], axis=0))
    ws = masked([jnp.exp2(zt - sp + tail) for zt, sp, tail in zip(zts, sps, tails)])
    pvs = [_dot(c[2], a.astype(BF16)) for c, a in zip(chains, ws)]
    totals = [tail[:1, :] - sp[:1, :] for tail, sp in zip(tails, sps)]
    return list(zip(pvs, totals))


def _sb_kernel(q_ref, k_ref, vt_ref, u_ref, g_ref, o_ref, *, tq, n_heads):
    qi = pl.program_id(2)
    pair = lambda h: slice((h // HEADS_PER_TILE) * LANES, (h // HEADS_PER_TILE + 1) * LANES)
    qh = [_mask_head(q_ref[:, pair(h)], h % HEADS_PER_TILE) for h in range(n_heads)]
    u2 = u_ref[...]
    key_i = lax.broadcasted_iota(jnp.int32, (tq, tq), 0)
    qry_i = lax.broadcasted_iota(jnp.int32, (tq, tq), 1)
    strict = key_i < qry_i

    def chains(j, mask):
        start = pl.multiple_of(j * tq, tq)
        return [(qh[h], k_ref[pl.ds(start, tq), pair(h)], vt_ref[j, pair(h), :], mask)
                for h in range(n_heads)]

    has_prev = qi > 0
    res = _sb_blocks(chains(qi, strict) + chains(jnp.maximum(qi - 1, 0), None), u2)
    state = []
    for h in range(n_heads):
        (pv_d, tot_d), (pv_p, tot_p) = res[h], res[n_heads + h]
        state += [pv_d + jnp.where(has_prev, jnp.exp2(tot_d), 0.0) * pv_p, tot_d + tot_p]

    def alive_after(st):
        top = functools.reduce(jnp.maximum, st[1::2])
        return jnp.max(top) > LOG2_DEAD

    def cond(carry):
        j, alive = carry[0], carry[1]
        return jnp.logical_and(j >= 0, alive)

    def body(carry):
        j, st = carry[0], carry[2:]
        res = _sb_blocks(chains(j, None), u2)
        out = []
        for h in range(n_heads):
            acc, run = st[2 * h], st[2 * h + 1]
            pv, total = res[h]
            out += [acc + jnp.exp2(run) * pv, run + total]
        return (j - 1, alive_after(out)) + tuple(out)

    fin = lax.while_loop(cond, body, (qi - 2, alive_after(state)) + tuple(state))[2:]
    row = lax.broadcasted_iota(jnp.int32, fin[0].shape, 0)
    for hp in range(n_heads // HEADS_PER_TILE):
        acc0, acc1 = fin[2 * (2 * hp)], fin[2 * (2 * hp + 1)]
        ot = jnp.where(row < HEAD_DIM, acc0, acc1)
        o_ref[:, pair(2 * hp)] = (ot.T * _silu(g_ref[:, pair(2 * hp)])).astype(o_ref.dtype)


def _sb_attention(p3, g3, vt, tq):
    b, s, _ = p3.shape
    nkb = s // tq
    half = tq // 2
    tri = np.triu(-np.ones((half, half), np.float32), 1)
    u2 = jnp.asarray(np.concatenate([tri, tri], axis=1), BF16)
    w = SB_HEADS_PER_STEP * HEAD_DIM
    return pl.pallas_call(
        functools.partial(_sb_kernel, tq=tq, n_heads=SB_HEADS_PER_STEP),
        out_shape=jax.ShapeDtypeStruct((b, s, WIDTH_A), BF16),
        grid=(b, HEADS_A // SB_HEADS_PER_STEP, nkb),
        in_specs=[
            pl.BlockSpec((None, tq, w), lambda bi, hp, qi: (bi, qi, P_QA // w + hp)),
            pl.BlockSpec((None, s, w), lambda bi, hp, qi: (bi, 0, P_KA // w + hp)),
            pl.BlockSpec((nkb, w, tq), lambda bi, hp, qi: (bi, T_VA // w + hp, 0)),
            pl.BlockSpec((half, 2 * half), lambda bi, hp, qi: (0, 0)),
            pl.BlockSpec((None, tq, w), lambda bi, hp, qi: (bi, qi, G_GA // w + hp)),
        ],
        out_specs=pl.BlockSpec((None, tq, w), lambda bi, hp, qi: (bi, qi, hp)),
        compiler_params=pltpu.CompilerParams(
            dimension_semantics=("arbitrary", "arbitrary", "arbitrary"),
            vmem_limit_bytes=VMEM_LIMIT_BYTES),
        name="sb_attention",
    )(p3, p3, vt, u2, g3)


def _dsa_kernel(rb_ref, far_ref, qi_ref, wi_ref, ki_ref, q_ref, k_ref, vt_ref, bkt_ref, g_ref,
                o_ref, sc_ref, hi_ref, eq_ref, bias_ref, m_ref, top_ref, acc_ref, *, tq, topk, seq):
    qi = pl.program_id(1)
    nblk = qi + 1

    @pl.when(jnp.logical_and(pl.program_id(0) == 0, qi == 0))
    def _():
        far = far_ref[0]
        for h in range(HEADS_B):
            for d in range(2):
                bkt = bkt_ref[d]

                def bucket_body(b, tile, bkt=bkt, h=h):
                    return jnp.where(bkt == b, (rb_ref[b, h] - rb_ref[far, h]) * LOG2E, tile)

                bias_ref[h, d] = lax.fori_loop(0, N_BUCKETS, bucket_body, jnp.zeros((tq, tq), F32))

    wi = wi_ref[...]
    qih = [_mask_head(qi_ref[:, (h // 2) * LANES:(h // 2 + 1) * LANES], h % 2)
           for h in range(IDX_HEADS)]

    def score_block(j, nb=1):
        start = pl.multiple_of(j * tq, tq)
        kib = ki_ref[pl.ds(start, tq * nb), :]
        rels = [_dot_nt(kib, qih[h]) for h in range(IDX_HEADS)]
        sc = jnp.zeros((tq * nb, tq), F32)
        for h in range(IDX_HEADS):
            sc = sc + jnp.maximum(rels[h], 0.0) * wi[h:h + 1, :]
        return sc

    def store_scores(j, sc):
        sc_ref[j] = sc
        hi_ref[j] = sc.astype(BF16)

    def score_group(j, nb):
        sc = score_block(j, nb)
        for i in range(nb):
            store_scores(j + i, sc[i * tq:(i + 1) * tq])

    def p1_body(i, c):
        score_group(P1_GROUP * i, P1_GROUP)
        return c

    n_groups1 = qi // P1_GROUP
    lax.fori_loop(0, n_groups1, p1_body, 0)
    done1 = n_groups1 * P1_GROUP
    size = P1_GROUP // 2
    while size >= 1:
        take = (qi // size) % 2 == 1

        @pl.when(take)
        def _(done1=done1, size=size):
            score_group(done1, size)

        done1 = done1 + jnp.where(take, size, 0)
        size //= 2

    key_i = lax.broadcasted_iota(jnp.int32, (tq, tq), 0)
    qry_i = lax.broadcasted_iota(jnp.int32, (tq, tq), 1)
    admissible = key_i < (qry_i // CHUNK + 1) * CHUNK
    store_scores(qi, jnp.where(admissible, score_block(qi), -jnp.inf))

    def count(pred):
        def body(i, acc):
            for j in (2 * i, 2 * i + 1):
                ones = jnp.where(pred(sc_ref[j], j * tq), 1.0, 0.0)
                acc = acc + _tree_sum([ones[r:r + SUBLANES, :] for r in range(0, tq, SUBLANES)])
            return acc
        acc = lax.fori_loop(0, (nblk + 1) // 2, body, jnp.zeros((SUBLANES, tq), F32))
        return jnp.sum(acc, axis=0, keepdims=True)

    def count16(ref, pred):
        rows = 2 * SUBLANES
        def body(i, acc):
            for blk in (ref[2 * i], ref[2 * i + 1]):
                ones = jnp.where(pred(blk), jnp.int16(1), jnp.int16(0))
                acc = acc + _tree_sum([ones[r:r + rows, :] for r in range(0, tq, rows)])
            return acc
        acc = lax.fori_loop(0, (nblk + 1) // 2, body, jnp.zeros((rows, tq), jnp.int16))
        return jnp.sum(acc.astype(F32), axis=0, keepdims=True)

    def count_hi(cand):
        return count16(hi_ref, lambda blk: blk >= cand)

    @pl.when(nblk % 2 == 1)
    def _():
        sc_ref[nblk] = jnp.full((tq, tq), -jnp.inf, F32)
        hi_ref[nblk] = jnp.full((tq, tq), -jnp.inf, BF16)

    def float_of_key(key):
        return lax.bitcast_convert_type(jnp.where(key < 0, key ^ 0x7FFFFFFF, key), F32)

    def key32_of_key16(key16):
        return key16 * (2 * HALF16) + jnp.where(key16 < 0, 2 * HALF16 - 1, 0)

    scanned = (nblk * tq).astype(F32)
    kf = float(topk)
    assert seq // (2 * SUBLANES) < 2 ** 15

    def hi_body(i, prefix):
        cand_u = prefix | jnp.left_shift(jnp.int32(1), 15 - i)
        key16 = cand_u - HALF16
        cand = float_of_key(key32_of_key16(key16)).astype(BF16)
        cnt = jnp.where(key16 <= NEG_INF_KEY16, scanned, count_hi(cand))
        return jnp.where(cnt >= kf, cand_u, prefix)

    hi_key = lax.fori_loop(0, 16, hi_body, jnp.zeros((1, tq), jnp.int32)) - HALF16
    finite_thr = hi_key > NEG_INF_KEY16
    base = jnp.where(finite_thr, key32_of_key16(hi_key) - HALF16, 0)
    base_val = float_of_key(base)
    base_cnt = count(lambda blk, k0: blk >= base_val)

    def lo_body(i, carry):
        off, cnt_ge, cnt_gt = carry
        off_try = off | jnp.left_shift(jnp.int32(1), 16 - i)
        cand = float_of_key(base + off_try)
        cnt = count(lambda blk, k0: blk >= cand)
        take = cnt >= kf
        return jnp.where(take, off_try, off), jnp.where(take, cnt, cnt_ge), jnp.where(take, cnt_gt, cnt)

    off, cnt_ge, cnt_gt = lax.fori_loop(
        0, 17, lo_body, (jnp.zeros((1, tq), jnp.int32), base_cnt, jnp.zeros((1, tq), F32)))
    thr = float_of_key(base + off)

    tie = jnp.logical_and(cnt_ge > kf, finite_thr)
    has_tie = jnp.max(jnp.where(tie, 1.0, 0.0)) > 0.0

    @pl.when(jnp.logical_not(has_tie))
    def _():
        thr_min = jnp.where(finite_thr, thr, MOST_NEGATIVE)

        def mask_body(j, c):
            sc_ref[j] = jnp.where(sc_ref[j] >= thr_min, 0.0, NEG_MASK)
            return c

        lax.fori_loop(0, nblk, mask_body, 0)

    @pl.when(has_tie)
    def _():
        need = kf - cnt_gt
        nbits = int(seq - 1).bit_length()
        assert seq < NO_INDEX

        def eq_body(j, c):
            idx = jnp.where(sc_ref[j] == thr, key_i + j * tq, NO_INDEX)
            eq_ref[j] = idx.astype(jnp.int16)
            return c

        lax.fori_loop(0, nblk, eq_body, 0)

        @pl.when(nblk % 2 == 1)
        def _():
            eq_ref[nblk] = jnp.full((tq, tq), NO_INDEX, jnp.int16)

        def idx_body(i, lo):
            c_try = lo + jnp.left_shift(jnp.int32(1), nbits - 1 - i)
            c16 = c_try.astype(jnp.int16)
            f = count16(eq_ref, lambda blk: blk < c16)
            return jnp.where(f < need, c_try, lo)

        lo = lax.fori_loop(0, nbits, idx_body, jnp.zeros((1, tq), jnp.int32))
        cut = jnp.where(tie, lo + 1, jnp.where(finite_thr, NO_CUT, 0))
        thr_gt = jnp.where(finite_thr, thr, -jnp.inf)

        def mask_body(j, c):
            blk = sc_ref[j]
            at_thr = jnp.where((key_i + j * tq) < cut, 0.0, NEG_MASK)
            sc_ref[j] = jnp.where(blk > thr_gt, 0.0, jnp.where(blk == thr_gt, at_thr, NEG_MASK))
            return c

        lax.fori_loop(0, nblk, mask_body, 0)

    qh = [_mask_head(q_ref[:, (h // 2) * LANES:(h // 2 + 1) * LANES], h % 2)
          for h in range(HEADS_B)]
    m_ref[...] = jnp.full(m_ref.shape, NEG_MASK, F32)
    acc_ref[...] = jnp.zeros(acc_ref.shape, F32)

    def att_block(j, nb, d, stream=False):
        start = pl.multiple_of(j * tq, tq)
        cat = lambda xs, axis: xs[0] if nb == 1 else jnp.concatenate(xs, axis=axis)
        mask = cat([sc_ref[j + i] for i in range(nb)], 0)
        ones = jnp.ones((DENOM_ROWS, tq * nb), BF16)
        vts = [jnp.concatenate(
            [cat([vt_ref[j + i, h * HEAD_DIM:(h + 1) * HEAD_DIM, :] for i in range(nb)], 1), ones], axis=0)
            for h in range(HEADS_B)]
        pair = lambda h: slice((h // HEADS_PER_TILE) * LANES, (h // HEADS_PER_TILE + 1) * LANES)
        qk = lambda h: _dot_nt(k_ref[pl.ds(start, tq * nb), pair(h)], qh[h])
        m_old = m_ref[...]
        hs = range(HEADS_B)

        def biased_logits(group):
            lgs = [qk(h) + mask for h in group]
            if d is not None:
                lgs = [lg + bias_ref[h, d] for lg, h in zip(lgs, group)]
            return lgs

        if stream:
            logits = biased_logits(hs)
            ps = [jnp.exp2(logits[h] - m_old[h:h + 1, :]).astype(BF16) for h in hs]
            bmax = jnp.concatenate([jnp.max(logits[h], axis=0, keepdims=True) for h in hs], axis=0)
            pvs = [_dot(vts[h], ps[h]) for h in hs]
            top_old = top_ref[...]
            accs = [acc_ref[h] + pvs[h] for h in hs]
            top_ref[...] = jnp.maximum(top_old, bmax)
            for h in hs:
                acc_ref[h] = accs[h]
            return
        logits = biased_logits(hs)
        m_new = [jnp.maximum(m_old[h:h + 1, :], jnp.max(logits[h], axis=0, keepdims=True)) for h in hs]
        ps = [jnp.exp2(logits[h] - m_new[h]).astype(BF16) for h in hs]
        pvs = [_dot(vts[h], ps[h]) for h in hs]
        accs = [jnp.exp2(m_old[h:h + 1, :] - m_new[h]) * acc_ref[h] + pvs[h] for h in hs]
        m_ref[...] = jnp.concatenate(m_new, axis=0)
        for h in range(HEADS_B):
            acc_ref[h] = accs[h]

    n_far = qi - 1

    att_block(qi, 1, 0)

    @pl.when(qi > 0)
    def _():
        att_block(qi - 1, 1, 1)

    top_ref[...] = m_ref[...]

    def far_body(i, c):
        att_block(FAR_GROUP * i, FAR_GROUP, None, stream=True)
        return c

    n_groups = jnp.maximum(n_far, 0) // FAR_GROUP
    lax.fori_loop(0, n_groups, far_body, 0)
    done = n_groups * FAR_GROUP
    size = FAR_GROUP // 2
    while size >= 1:
        take = jnp.logical_and(n_far > 0, (n_far // size) % 2 == 1)

        @pl.when(take)
        def _(done=done, size=size):
            att_block(done, size, None, stream=True)

        done = done + jnp.where(take, size, 0)
        size //= 2

    @pl.when(jnp.max(top_ref[...] - m_ref[...]) > SOFTMAX_SLACK)
    def _():
        m_ref[...] = jnp.full(m_ref.shape, NEG_MASK, F32)
        acc_ref[...] = jnp.zeros(acc_ref.shape, F32)

        def exact_far(j, c):
            att_block(j, 1, None)
            return c

        lax.fori_loop(0, n_far, exact_far, 0)

        @pl.when(qi > 0)
        def _():
            att_block(qi - 1, 1, 1)

        att_block(qi, 1, 0)

    for hp in range(HEADS_B // HEADS_PER_TILE):
        h0 = hp * HEADS_PER_TILE
        a0, a1 = acc_ref[h0], acc_ref[h0 + 1]
        ot = jnp.concatenate([a0[:HEAD_DIM] / a0[HEAD_DIM:HEAD_DIM + 1, :],
                              a1[:HEAD_DIM] / a1[HEAD_DIM:HEAD_DIM + 1, :]], axis=0)
        gp = g_ref[:, hp * LANES:(hp + 1) * LANES]
        o_ref[:, hp * LANES:(hp + 1) * LANES] = (ot.T * _silu(gp)).astype(o_ref.dtype)


def _t5_bucket(rel):
    half = N_BUCKETS // 2
    max_exact = half // 2
    ret = jnp.where(rel > 0, half, 0)
    n = jnp.abs(rel)
    nf = jnp.maximum(n, 1).astype(jnp.float32)
    large = max_exact + (jnp.log(nf / max_exact) / math.log(MAX_DISTANCE / max_exact)
                         * (half - max_exact)).astype(jnp.int32)
    large = jnp.minimum(large, half - 1)
    return ret + jnp.where(n < max_exact, n, large)


def _dsa_attention(p3, g3, vt, wi_t, rel_bias, tq, topk):
    b, s, _ = p3.shape
    nkb = s // tq
    assert tq >= MAX_DISTANCE
    key = jnp.arange(tq, dtype=jnp.int32)[:, None]
    qry = jnp.arange(tq, dtype=jnp.int32)[None, :]
    buckets = jnp.stack([_t5_bucket(key - qry), _t5_bucket(key - qry - tq)])
    far = _t5_bucket(jnp.full((1,), -tq - 1, jnp.int32))
    buckets = jnp.bitwise_and(buckets, 2 * N_BUCKETS - 1)
    far = jnp.bitwise_and(far, 2 * N_BUCKETS - 1)
    row = lambda blk: (lambda bi, qi: (bi, qi, blk))
    full = lambda blk: (lambda bi, qi: (bi, 0, blk))
    smem = pl.BlockSpec(memory_space=pltpu.SMEM)
    return pl.pallas_call(
        functools.partial(_dsa_kernel, tq=tq, topk=topk, seq=s),
        out_shape=jax.ShapeDtypeStruct((b, s, WIDTH_B), BF16),
        grid=(b, nkb),
        in_specs=[
            smem, smem,
            pl.BlockSpec((None, tq, 512), row(P_QI // 512)),
            pl.BlockSpec((T_WI_ROWS, tq), lambda bi, qi: (0, bi * nkb + qi)),
            pl.BlockSpec((None, s, LANES), full(P_KI // LANES)),
            pl.BlockSpec((None, tq, 512), row(P_QB // 512)),
            pl.BlockSpec((None, s, 512), full(P_KB // 512)),
            pl.BlockSpec((nkb, WIDTH_B, tq), lambda bi, qi: (bi, T_VB // WIDTH_B, 0)),
            pl.BlockSpec((2, tq, tq), lambda bi, qi: (0, 0, 0)),
            pl.BlockSpec((None, tq, 512), row(G_GB // 512)),
        ],
        out_specs=pl.BlockSpec((None, tq, WIDTH_B), lambda bi, qi: (bi, qi, 0)),
        scratch_shapes=[pltpu.VMEM((nkb + nkb % 2, tq, tq), F32),
                        pltpu.VMEM((nkb + nkb % 2, tq, tq), BF16),
                        pltpu.VMEM((nkb + nkb % 2, tq, tq), jnp.int16),
                        pltpu.VMEM((HEADS_B, 2, tq, tq), F32),
                        pltpu.VMEM((HEADS_B, tq), F32),
                        pltpu.VMEM((HEADS_B, tq), F32),
                        pltpu.VMEM((HEADS_B, HEAD_DIM + DENOM_ROWS, tq), F32)],
        compiler_params=pltpu.CompilerParams(
            dimension_semantics=("arbitrary", "arbitrary"),
            vmem_limit_bytes=VMEM_LIMIT_BYTES),
        name="dsa_attention",
    )(rel_bias, far, p3, wi_t, p3, p3, p3, vt, buckets, g3)


def _out_kernel(ma_ref, mb_ref, x_ref, w_ref, lg_ref, lb_ref, o_ref, *, alpha):
    y = _dot(ma_ref[...], w_ref[:WIDTH_A, :]) + _dot(mb_ref[...], w_ref[WIDTH_A:, :])
    h = alpha * x_ref[...] + y
    mu = jnp.mean(h, axis=-1, keepdims=True)
    hc = h - mu
    var = jnp.mean(hc * hc, axis=-1, keepdims=True)
    o_ref[...] = hc * lax.rsqrt(var + LN_EPS) * lg_ref[...] + lb_ref[...]


def _out_proj_ln(mix_a, mix_b, x2, w_out, ln_g, ln_b, alpha, tm):
    m, d = x2.shape
    return pl.pallas_call(
        functools.partial(_out_kernel, alpha=alpha),
        out_shape=jax.ShapeDtypeStruct((m, d), F32),
        grid=(m // tm,),
        in_specs=[pl.BlockSpec((tm, WIDTH_A), lambda i: (i, 0)),
                  pl.BlockSpec((tm, WIDTH_B), lambda i: (i, 0)),
                  pl.BlockSpec((tm, d), lambda i: (i, 0)),
                  pl.BlockSpec((WIDTH_A + WIDTH_B, d), lambda i: (0, 0)),
                  pl.BlockSpec((1, d), lambda i: (0, 0)),
                  pl.BlockSpec((1, d), lambda i: (0, 0))],
        out_specs=pl.BlockSpec((tm, d), lambda i: (i, 0)),
        compiler_params=pltpu.CompilerParams(
            dimension_semantics=("arbitrary",), vmem_limit_bytes=VMEM_LIMIT_BYTES),
        name="out_proj_ln",
    )(mix_a, mix_b, x2, w_out, ln_g, ln_b)


def _pack_w_in(w):
    scale = LOG2E / math.sqrt(HEAD_DIM)
    sizes = (WIDTH_A,) * 4 + (WIDTH_B,) * 4 + (IDX_HEADS * IDX_DIM, IDX_DIM, IDX_HEADS)
    qa, ka, va, ga, qb, kb, vb, gb, qi, ki, wi = jnp.split(w, np.cumsum(sizes)[:-1].tolist(), axis=1)
    w_row = jnp.concatenate([qa * scale, ka, qb * scale, kb, qi, ki, ki, ga, gb], axis=1)
    wi_pad = jnp.pad(wi, ((0, 0), (0, T_WI_ROWS - IDX_HEADS)))
    w_t = jnp.concatenate([va, vb, wi_pad], axis=1).T
    return w_row.astype(BF16), w_t.astype(BF16)


def _layer(x, w_in, w_out, ln_g, ln_b, rel_bias, alpha, topk):
    b, s, d = x.shape
    tq = min(256, s)
    tm = min(512, b * s)
    x2 = x.reshape(b * s, d)
    w_row, w_t = _pack_w_in(w_in)
    p2, g2, vt, wi_t = _proj(x2, w_row, w_t, tm=tm, tk=tq)
    p3 = p2.reshape(b, s, P_COLS)
    g3 = g2.reshape(b, s, G_COLS)
    mix_a = _sb_attention(p3, g3, vt, tq)
    mix_b = _dsa_attention(p3, g3, vt, wi_t, rel_bias, tq, topk)
    out = _out_proj_ln(mix_a.reshape(b * s, WIDTH_A), mix_b.reshape(b * s, WIDTH_B), x2,
                       w_out.astype(BF16), ln_g.reshape(1, d), ln_b.reshape(1, d), alpha, tm=tm)
    return out.reshape(b, s, d)


def kernel(x, w_in, w_out, ln_g, ln_b, rel_bias):
    depth = w_in.shape[0]
    alpha = (2.0 * depth) ** 0.25
    topk = min(TOPK_MAX, x.shape[1] // 4)
    h = x
    for layer in range(depth):
        h = _layer(h, w_in[layer], w_out[layer], ln_g[layer], ln_b[layer], rel_bias, alpha, topk)
    return h
```

```python
import functools
import math

import jax
import jax.numpy as jnp
import numpy as np
from jax import lax
from jax.experimental import pallas as pl
from jax.experimental.pallas import tpu as pltpu

HEAD_DIM = 64
HEADS_A = 8
HEADS_B = 8
WIDTH_A = HEADS_A * HEAD_DIM
WIDTH_B = HEADS_B * HEAD_DIM
IDX_HEADS = 8
IDX_DIM = 64
CHUNK = 64
TOPK_MAX = 256
N_BUCKETS = 32
MAX_DISTANCE = 128
LN_EPS = 1e-5

LANES = 128
SUBLANES = 8
HEADS_PER_TILE = LANES // HEAD_DIM
VMEM_LIMIT_BYTES = 56 * 1024 * 1024

P_QA, P_KA, P_QB, P_KB, P_QI = (i * 512 for i in range(5))
P_KI = 5 * 512
P_COLS = P_KI + LANES
G_GA, G_GB = 0, 512
G_COLS = 1024
T_VA, T_VB, T_WI = 0, 512, 1024
T_WI_ROWS = 16
T_ROWS = T_WI + T_WI_ROWS

NEG_MASK = -1e30
MOST_NEGATIVE = float(np.finfo(np.float32).min)
NEG_INF_KEY16 = -32641
NO_CUT = 2 ** 30
NO_INDEX = 2 ** 15 - 1
HALF16 = 2 ** 15
SOFTMAX_SLACK = 40.0
LOG2E = math.log2(math.e)
DENOM_ROWS = 16
P1_GROUP = 4
FAR_GROUP = 4
SB_HEADS_PER_STEP = 8
LOG2_DEAD = -151.0

BF16 = jnp.bfloat16
F32 = jnp.float32


def _dot_nt(a, b):
    return lax.dot_general(a, b, (((1,), (1,)), ((), ())), preferred_element_type=F32)


def _dot(a, b):
    return jnp.dot(a, b, preferred_element_type=F32)


def _mask_head(x, parity):
    lane = lax.broadcasted_iota(jnp.int32, x.shape, 1)
    keep = (lane < HEAD_DIM) if parity == 0 else (lane >= HEAD_DIM)
    return jnp.where(keep, x, jnp.zeros_like(x))


def _tree_sum(xs):
    while len(xs) > 1:
        xs = [a + b for a, b in zip(xs[::2], xs[1::2])] + ([xs[-1]] if len(xs) % 2 else [])
    return xs[0]


def _silu(g):
    return g / (1.0 + jnp.exp(-g))


def _proj_kernel(x_ref, wr_ref, wt_ref, p_ref, g_ref, vt_ref, wi_ref, *, tk):
    xb = x_ref[...].astype(BF16)
    for c in range(0, P_COLS, 512):
        cw = min(512, P_COLS - c)
        p_ref[:, c:c + cw] = _dot(xb, wr_ref[:, c:c + cw]).astype(BF16)
    for c in range(0, G_COLS, 512):
        g_ref[:, c:c + 512] = _dot(xb, wr_ref[:, P_COLS + c:P_COLS + c + 512])
    for r in range(0, T_WI, 512):
        t = _dot_nt(wt_ref[r:r + 512, :], xb).astype(BF16)
        for kb in range(xb.shape[0] // tk):
            vt_ref[kb, r:r + 512, :] = t[:, kb * tk:(kb + 1) * tk]
    wi_ref[...] = _dot_nt(wt_ref[T_WI:, :], xb)


def _proj(x2, w_row, w_t, tm, tk):
    m, d = x2.shape
    return pl.pallas_call(
        functools.partial(_proj_kernel, tk=tk),
        out_shape=(jax.ShapeDtypeStruct((m, P_COLS), BF16),
                   jax.ShapeDtypeStruct((m, G_COLS), F32),
                   jax.ShapeDtypeStruct((m // tk, T_WI, tk), BF16),
                   jax.ShapeDtypeStruct((T_WI_ROWS, m), F32)),
        grid=(m // tm,),
        in_specs=[pl.BlockSpec((tm, d), lambda i: (i, 0)),
                  pl.BlockSpec((d, P_COLS + G_COLS), lambda i: (0, 0)),
                  pl.BlockSpec((T_ROWS, d), lambda i: (0, 0))],
        out_specs=(pl.BlockSpec((tm, P_COLS), lambda i: (i, 0)),
                   pl.BlockSpec((tm, G_COLS), lambda i: (i, 0)),
                   pl.BlockSpec((tm // tk, T_WI, tk), lambda i: (i, 0, 0)),
                   pl.BlockSpec((T_WI_ROWS, tm), lambda i: (0, i))),
        compiler_params=pltpu.CompilerParams(
            dimension_semantics=("arbitrary",), vmem_limit_bytes=VMEM_LIMIT_BYTES),
        name="proj",
    )(x2, w_row, w_t)


def _softplus2(z):
    return jnp.maximum(z, 0.0) + jnp.log2(1.0 + jnp.exp2(-jnp.abs(z)))


def _sb_blocks(chains, u2):
    half = u2.shape[0]
    stricts = [c[3] for c in chains]
    masked = lambda xs: [x if m is None else jnp.where(m, x, 0.0) for x, m in zip(xs, stricts)]
    zts = [_dot_nt(kblk, qh) for qh, kblk, _, _ in chains]
    sps = masked([_softplus2(zt) for zt in zts])
    his = [sp.astype(BF16) for sp in sps]
    los = [(sp - hi.astype(F32)).astype(BF16) for sp, hi in zip(sps, his)]
    halves = lambda x: (x[:half], x[half:])
    tails = []
    for sp, hi, lo in zip(sps, his, los):
        (hi_a, hi_b), (lo_a, lo_b) = halves(hi), halves(lo)
        tail_a = _dot(u2, jnp.concatenate([hi_a, lo_a], axis=0))
        tail_b = _dot(u2, jnp.concatenate([hi_b, lo_b], axis=0))
        total_b = tail_b[:1, :] - sp[half:half + 1, :]
        tails.append(jnp.concatenate([tail_a + total_b, tail_b], axis=0))
    ws = masked([jnp.exp2(zt - sp + tail) for zt, sp, tail in zip(zts, sps, tails)])
    pvs = [_dot(c[2], a.astype(BF16)) for c, a in zip(chains, ws)]
    totals = [tail[:1, :] - sp[:1, :] for tail, sp in zip(tails, sps)]
    return list(zip(pvs, totals))


def _sb_kernel(q_ref, k_ref, vt_ref, u_ref, g_ref, o_ref, *, tq, n_heads):
    qi = pl.program_id(2)
    pair = lambda h: slice((h // HEADS_PER_TILE) * LANES, (h // HEADS_PER_TILE + 1) * LANES)
    qh = [_mask_head(q_ref[:, pair(h)], h % HEADS_PER_TILE) for h in range(n_heads)]
    u2 = u_ref[...]
    key_i = lax.broadcasted_iota(jnp.int32, (tq, tq), 0)
    qry_i = lax.broadcasted_iota(jnp.int32, (tq, tq), 1)
    strict = key_i < qry_i

    def chains(j, mask):
        start = pl.multiple_of(j * tq, tq)
        return [(qh[h], k_ref[pl.ds(start, tq), pair(h)], vt_ref[j, pair(h), :], mask)
                for h in range(n_heads)]

    has_prev = qi > 0
    res = _sb_blocks(chains(qi, strict) + chains(jnp.maximum(qi - 1, 0), None), u2)
    state = []
    for h in range(n_heads):
        (pv_d, tot_d), (pv_p, tot_p) = res[h], res[n_heads + h]
        state += [pv_d + jnp.where(has_prev, jnp.exp2(tot_d), 0.0) * pv_p, tot_d + tot_p]

    def alive_after(st):
        top = functools.reduce(jnp.maximum, st[1::2])
        return jnp.max(top) > LOG2_DEAD

    def cond(carry):
        j, alive = carry[0], carry[1]
        return jnp.logical_and(j >= 0, alive)

    def body(carry):
        j, st = carry[0], carry[2:]
        res = _sb_blocks(chains(j, None), u2)
        out = []
        for h in range(n_heads):
            acc, run = st[2 * h], st[2 * h + 1]
            pv, total = res[h]
            out += [acc + jnp.exp2(run) * pv, run + total]
        return (j - 1, alive_after(out)) + tuple(out)

    fin = lax.while_loop(cond, body, (qi - 2, alive_after(state)) + tuple(state))[2:]
    row = lax.broadcasted_iota(jnp.int32, fin[0].shape, 0)
    for hp in range(n_heads // HEADS_PER_TILE):
        acc0, acc1 = fin[2 * (2 * hp)], fin[2 * (2 * hp + 1)]
        ot = jnp.where(row < HEAD_DIM, acc0, acc1)
        o_ref[:, pair(2 * hp)] = (ot.T * _silu(g_ref[:, pair(2 * hp)])).astype(o_ref.dtype)


def _sb_attention(p3, g3, vt, tq):
    b, s, _ = p3.shape
    nkb = s // tq
    half = tq // 2
    tri = np.triu(-np.ones((half, half), np.float32), 1)
    u2 = jnp.asarray(np.concatenate([tri, tri], axis=1), BF16)
    w = SB_HEADS_PER_STEP * HEAD_DIM
    return pl.pallas_call(
        functools.partial(_sb_kernel, tq=tq, n_heads=SB_HEADS_PER_STEP),
        out_shape=jax.ShapeDtypeStruct((b, s, WIDTH_A), BF16),
        grid=(b, HEADS_A // SB_HEADS_PER_STEP, nkb),
        in_specs=[
            pl.BlockSpec((None, tq, w), lambda bi, hp, qi: (bi, qi, P_QA // w + hp)),
            pl.BlockSpec((None, s, w), lambda bi, hp, qi: (bi, 0, P_KA // w + hp)),
            pl.BlockSpec((nkb, w, tq), lambda bi, hp, qi: (bi, T_VA // w + hp, 0)),
            pl.BlockSpec((half, 2 * half), lambda bi, hp, qi: (0, 0)),
            pl.BlockSpec((None, tq, w), lambda bi, hp, qi: (bi, qi, G_GA // w + hp)),
        ],
        out_specs=pl.BlockSpec((None, tq, w), lambda bi, hp, qi: (bi, qi, hp)),
        compiler_params=pltpu.CompilerParams(
            dimension_semantics=("arbitrary", "arbitrary", "arbitrary"),
            vmem_limit_bytes=VMEM_LIMIT_BYTES),
        name="sb_attention",
    )(p3, p3, vt, u2, g3)


def _dsa_kernel(rb_ref, far_ref, qi_ref, wi_ref, ki_ref, q_ref, k_ref, vt_ref, bkt_ref, g_ref,
                o_ref, sc_ref, hi_ref, eq_ref, bias_ref, m_ref, top_ref, acc_ref, *, tq, topk, seq):
    qi = pl.program_id(1)
    nblk = qi + 1

    @pl.when(jnp.logical_and(pl.program_id(0) == 0, qi == 0))
    def _():
        far = far_ref[0]
        for h in range(HEADS_B):
            for d in range(2):
                bkt = bkt_ref[d]

                def bucket_body(b, tile, bkt=bkt, h=h):
                    return jnp.where(bkt == b, (rb_ref[b, h] - rb_ref[far, h]) * LOG2E, tile)

                bias_ref[h, d] = lax.fori_loop(0, N_BUCKETS, bucket_body, jnp.zeros((tq, tq), F32))

    wi = wi_ref[...]
    qih = [_mask_head(qi_ref[:, (h // 2) * LANES:(h // 2 + 1) * LANES], h % 2)
           for h in range(IDX_HEADS)]

    def score_block(j, nb=1):
        start = pl.multiple_of(j * tq, tq)
        kib = ki_ref[pl.ds(start, tq * nb), :]
        rels = [_dot_nt(kib, qih[h]) for h in range(IDX_HEADS)]
        sc = jnp.zeros((tq * nb, tq), F32)
        for h in range(IDX_HEADS):
            sc = sc + jnp.maximum(rels[h], 0.0) * wi[h:h + 1, :]
        return sc

    def store_scores(j, sc):
        sc_ref[j] = sc
        hi_ref[j] = sc.astype(BF16)

    def score_group(j, nb):
        sc = score_block(j, nb)
        for i in range(nb):
            store_scores(j + i, sc[i * tq:(i + 1) * tq])

    def p1_body(i, c):
        score_group(P1_GROUP * i, P1_GROUP)
        return c

    n_groups1 = qi // P1_GROUP
    lax.fori_loop(0, n_groups1, p1_body, 0)
    done1 = n_groups1 * P1_GROUP
    size = P1_GROUP // 2
    while size >= 1:
        take = (qi // size) % 2 == 1

        @pl.when(take)
        def _(done1=done1, size=size):
            score_group(done1, size)

        done1 = done1 + jnp.where(take, size, 0)
        size //= 2

    key_i = lax.broadcasted_iota(jnp.int32, (tq, tq), 0)
    qry_i = lax.broadcasted_iota(jnp.int32, (tq, tq), 1)
    admissible = key_i < (qry_i // CHUNK + 1) * CHUNK
    store_scores(qi, jnp.where(admissible, score_block(qi), -jnp.inf))

    def count(pred):
        def body(i, acc):
            for j in (2 * i, 2 * i + 1):
                ones = jnp.where(pred(sc_ref[j], j * tq), 1.0, 0.0)
                acc = acc + _tree_sum([ones[r:r + SUBLANES, :] for r in range(0, tq, SUBLANES)])
            return acc
        acc = lax.fori_loop(0, (nblk + 1) // 2, body, jnp.zeros((SUBLANES, tq), F32))
        return jnp.sum(acc, axis=0, keepdims=True)

    def count16(ref, pred):
        rows = 2 * SUBLANES
        def body(i, acc):
            for blk in (ref[2 * i], ref[2 * i + 1]):
                ones = jnp.where(pred(blk), jnp.int16(1), jnp.int16(0))
                acc = acc + _tree_sum([ones[r:r + rows, :] for r in range(0, tq, rows)])
            return acc
        acc = lax.fori_loop(0, (nblk + 1) // 2, body, jnp.zeros((rows, tq), jnp.int16))
        return jnp.sum(acc.astype(F32), axis=0, keepdims=True)

    def count_hi(cand):
        return count16(hi_ref, lambda blk: blk >= cand)

    @pl.when(nblk % 2 == 1)
    def _():
        sc_ref[nblk] = jnp.full((tq, tq), -jnp.inf, F32)
        hi_ref[nblk] = jnp.full((tq, tq), -jnp.inf, BF16)

    def float_of_key(key):
        return lax.bitcast_convert_type(jnp.where(key < 0, key ^ 0x7FFFFFFF, key), F32)

    def key32_of_key16(key16):
        return key16 * (2 * HALF16) + jnp.where(key16 < 0, 2 * HALF16 - 1, 0)

    scanned = (nblk * tq).astype(F32)
    kf = float(topk)
    assert seq // (2 * SUBLANES) < 2 ** 15

    def hi_body(i, prefix):
        cand_u = prefix | jnp.left_shift(jnp.int32(1), 15 - i)
        key16 = cand_u - HALF16
        cand = float_of_key(key32_of_key16(key16)).astype(BF16)
        cnt = jnp.where(key16 <= NEG_INF_KEY16, scanned, count_hi(cand))
        return jnp.where(cnt >= kf, cand_u, prefix)

    hi_key = lax.fori_loop(0, 16, hi_body, jnp.zeros((1, tq), jnp.int32)) - HALF16
    finite_thr = hi_key > NEG_INF_KEY16
    base = jnp.where(finite_thr, key32_of_key16(hi_key) - HALF16, 0)
    base_val = float_of_key(base)
    base_cnt = count(lambda blk, k0: blk >= base_val)

    def lo_body(i, carry):
        off, cnt_ge, cnt_gt = carry
        off_try = off | jnp.left_shift(jnp.int32(1), 16 - i)
        cand = float_of_key(base + off_try)
        cnt = count(lambda blk, k0: blk >= cand)
        take = cnt >= kf
        return jnp.where(take, off_try, off), jnp.where(take, cnt, cnt_ge), jnp.where(take, cnt_gt, cnt)

    off, cnt_ge, cnt_gt = lax.fori_loop(
        0, 17, lo_body, (jnp.zeros((1, tq), jnp.int32), base_cnt, jnp.zeros((1, tq), F32)))
    thr = float_of_key(base + off)

    tie = jnp.logical_and(cnt_ge > kf, finite_thr)
    has_tie = jnp.max(jnp.where(tie, 1.0, 0.0)) > 0.0

    @pl.when(jnp.logical_not(has_tie))
    def _():
        thr_min = jnp.where(finite_thr, thr, MOST_NEGATIVE)

        def mask_body(j, c):
            sc_ref[j] = jnp.where(sc_ref[j] >= thr_min, 0.0, NEG_MASK)
            return c

        lax.fori_loop(0, nblk, mask_body, 0)

    @pl.when(has_tie)
    def _():
        need = kf - cnt_gt
        nbits = int(seq - 1).bit_length()
        assert seq < NO_INDEX

        def eq_body(j, c):
            idx = jnp.where(sc_ref[j] == thr, key_i + j * tq, NO_INDEX)
            eq_ref[j] = idx.astype(jnp.int16)
            return c

        lax.fori_loop(0, nblk, eq_body, 0)

        @pl.when(nblk % 2 == 1)
        def _():
            eq_ref[nblk] = jnp.full((tq, tq), NO_INDEX, jnp.int16)

        def idx_body(i, lo):
            c_try = lo + jnp.left_shift(jnp.int32(1), nbits - 1 - i)
            c16 = c_try.astype(jnp.int16)
            f = count16(eq_ref, lambda blk: blk < c16)
            return jnp.where(f < need, c_try, lo)

        lo = lax.fori_loop(0, nbits, idx_body, jnp.zeros((1, tq), jnp.int32))
        cut = jnp.where(tie, lo + 1, jnp.where(finite_thr, NO_CUT, 0))
        thr_gt = jnp.where(finite_thr, thr, -jnp.inf)

        def mask_body(j, c):
            blk = sc_ref[j]
            at_thr = jnp.where((key_i + j * tq) < cut, 0.0, NEG_MASK)
            sc_ref[j] = jnp.where(blk > thr_gt, 0.0, jnp.where(blk == thr_gt, at_thr, NEG_MASK))
            return c

        lax.fori_loop(0, nblk, mask_body, 0)

    qh = [_mask_head(q_ref[:, (h // 2) * LANES:(h // 2 + 1) * LANES], h % 2)
          for h in range(HEADS_B)]
    m_ref[...] = jnp.full(m_ref.shape, NEG_MASK, F32)
    acc_ref[...] = jnp.zeros(acc_ref.shape, F32)

    def att_block(j, nb, d, stream=False):
        start = pl.multiple_of(j * tq, tq)
        cat = lambda xs, axis: xs[0] if nb == 1 else jnp.concatenate(xs, axis=axis)
        mask = cat([sc_ref[j + i] for i in range(nb)], 0)
        ones = jnp.ones((DENOM_ROWS, tq * nb), BF16)
        vts = [jnp.concatenate(
            [cat([vt_ref[j + i, h * HEAD_DIM:(h + 1) * HEAD_DIM, :] for i in range(nb)], 1), ones], axis=0)
            for h in range(HEADS_B)]
        pair = lambda h: slice((h // HEADS_PER_TILE) * LANES, (h // HEADS_PER_TILE + 1) * LANES)
        qk = lambda h: _dot_nt(k_ref[pl.ds(start, tq * nb), pair(h)], qh[h])
        m_old = m_ref[...]
        hs = range(HEADS_B)

        def biased_logits(group):
            lgs = [qk(h) + mask for h in group]
            if d is not None:
                lgs = [lg + bias_ref[h, d] for lg, h in zip(lgs, group)]
            return lgs

        if stream:
            logits = biased_logits(hs)
            ps = [jnp.exp2(logits[h] - m_old[h:h + 1, :]).astype(BF16) for h in hs]
            bmax = jnp.concatenate([jnp.max(logits[h], axis=0, keepdims=True) for h in hs], axis=0)
            pvs = [_dot(vts[h], ps[h]) for h in hs]
            top_old = top_ref[...]
            accs = [acc_ref[h] + pvs[h] for h in hs]
            top_ref[...] = jnp.maximum(top_old, bmax)
            for h in hs:
                acc_ref[h] = accs[h]
            return
        logits = biased_logits(hs)
        m_new = [jnp.maximum(m_old[h:h + 1, :], jnp.max(logits[h], axis=0, keepdims=True)) for h in hs]
        ps = [jnp.exp2(logits[h] - m_new[h]).astype(BF16) for h in hs]
        pvs = [_dot(vts[h], ps[h]) for h in hs]
        accs = [jnp.exp2(m_old[h:h + 1, :] - m_new[h]) * acc_ref[h] + pvs[h] for h in hs]
        m_ref[...] = jnp.concatenate(m_new, axis=0)
        for h in range(HEADS_B):
            acc_ref[h] = accs[h]

    n_far = qi - 1

    att_block(qi, 1, 0)

    @pl.when(qi > 0)
    def _():
        att_block(qi - 1, 1, 1)

    top_ref[...] = m_ref[...]

    def far_body(i, c):
        att_block(FAR_GROUP * i, FAR_GROUP, None, stream=True)
        return c

    n_groups = jnp.maximum(n_far, 0) // FAR_GROUP
    lax.fori_loop(0, n_groups, far_body, 0)
    done = n_groups * FAR_GROUP
    size = FAR_GROUP // 2
    while size >= 1:
        take = jnp.logical_and(n_far > 0, (n_far // size) % 2 == 1)

        @pl.when(take)
        def _(done=done, size=size):
            att_block(done, size, None, stream=True)

        done = done + jnp.where(take, size, 0)
        size //= 2

    @pl.when(jnp.max(top_ref[...] - m_ref[...]) > SOFTMAX_SLACK)
    def _():
        m_ref[...] = jnp.full(m_ref.shape, NEG_MASK, F32)
        acc_ref[...] = jnp.zeros(acc_ref.shape, F32)

        def exact_far(j, c):
            att_block(j, 1, None)
            return c

        lax.fori_loop(0, n_far, exact_far, 0)

        @pl.when(qi > 0)
        def _():
            att_block(qi - 1, 1, 1)

        att_block(qi, 1, 0)

    for hp in range(HEADS_B // HEADS_PER_TILE):
        h0 = hp * HEADS_PER_TILE
        a0, a1 = acc_ref[h0], acc_ref[h0 + 1]
        ot = jnp.concatenate([a0[:HEAD_DIM] / a0[HEAD_DIM:HEAD_DIM + 1, :],
                              a1[:HEAD_DIM] / a1[HEAD_DIM:HEAD_DIM + 1, :]], axis=0)
        gp = g_ref[:, hp * LANES:(hp + 1) * LANES]
        o_ref[:, hp * LANES:(hp + 1) * LANES] = (ot.T * _silu(gp)).astype(o_ref.dtype)


def _t5_bucket(rel):
    half = N_BUCKETS // 2
    max_exact = half // 2
    ret = jnp.where(rel > 0, half, 0)
    n = jnp.abs(rel)
    nf = jnp.maximum(n, 1).astype(jnp.float32)
    large = max_exact + (jnp.log(nf / max_exact) / math.log(MAX_DISTANCE / max_exact)
                         * (half - max_exact)).astype(jnp.int32)
    large = jnp.minimum(large, half - 1)
    return ret + jnp.where(n < max_exact, n, large)


def _dsa_attention(p3, g3, vt, wi_t, rel_bias, tq, topk):
    b, s, _ = p3.shape
    nkb = s // tq
    assert tq >= MAX_DISTANCE
    key = jnp.arange(tq, dtype=jnp.int32)[:, None]
    qry = jnp.arange(tq, dtype=jnp.int32)[None, :]
    buckets = jnp.stack([_t5_bucket(key - qry), _t5_bucket(key - qry - tq)])
    far = _t5_bucket(jnp.full((1,), -tq - 1, jnp.int32))
    buckets = jnp.bitwise_and(buckets, 2 * N_BUCKETS - 1)
    far = jnp.bitwise_and(far, 2 * N_BUCKETS - 1)
    row = lambda blk: (lambda bi, qi: (bi, qi, blk))
    full = lambda blk: (lambda bi, qi: (bi, 0, blk))
    smem = pl.BlockSpec(memory_space=pltpu.SMEM)
    return pl.pallas_call(
        functools.partial(_dsa_kernel, tq=tq, topk=topk, seq=s),
        out_shape=jax.ShapeDtypeStruct((b, s, WIDTH_B), BF16),
        grid=(b, nkb),
        in_specs=[
            smem, smem,
            pl.BlockSpec((None, tq, 512), row(P_QI // 512)),
            pl.BlockSpec((T_WI_ROWS, tq), lambda bi, qi: (0, bi * nkb + qi)),
            pl.BlockSpec((None, s, LANES), full(P_KI // LANES)),
            pl.BlockSpec((None, tq, 512), row(P_QB // 512)),
            pl.BlockSpec((None, s, 512), full(P_KB // 512)),
            pl.BlockSpec((nkb, WIDTH_B, tq), lambda bi, qi: (bi, T_VB // WIDTH_B, 0)),
            pl.BlockSpec((2, tq, tq), lambda bi, qi: (0, 0, 0)),
            pl.BlockSpec((None, tq, 512), row(G_GB // 512)),
        ],
        out_specs=pl.BlockSpec((None, tq, WIDTH_B), lambda bi, qi: (bi, qi, 0)),
        scratch_shapes=[pltpu.VMEM((nkb + nkb % 2, tq, tq), F32),
                        pltpu.VMEM((nkb + nkb % 2, tq, tq), BF16),
                        pltpu.VMEM((nkb + nkb % 2, tq, tq), jnp.int16),
                        pltpu.VMEM((HEADS_B, 2, tq, tq), F32),
                        pltpu.VMEM((HEADS_B, tq), F32),
                        pltpu.VMEM((HEADS_B, tq), F32),
                        pltpu.VMEM((HEADS_B, HEAD_DIM + DENOM_ROWS, tq), F32)],
        compiler_params=pltpu.CompilerParams(
            dimension_semantics=("arbitrary", "arbitrary"),
            vmem_limit_bytes=VMEM_LIMIT_BYTES),
        name="dsa_attention",
    )(rel_bias, far, p3, wi_t, p3, p3, p3, vt, buckets, g3)


def _out_kernel(ma_ref, mb_ref, x_ref, w_ref, lg_ref, lb_ref, o_ref, *, alpha):
    y = _dot(ma_ref[...], w_ref[:WIDTH_A, :]) + _dot(mb_ref[...], w_ref[WIDTH_A:, :])
    h = alpha * x_ref[...] + y
    mu = jnp.mean(h, axis=-1, keepdims=True)
    hc = h - mu
    var = jnp.mean(hc * hc, axis=-1, keepdims=True)
    o_ref[...] = hc * lax.rsqrt(var + LN_EPS) * lg_ref[...] + lb_ref[...]


def _out_proj_ln(mix_a, mix_b, x2, w_out, ln_g, ln_b, alpha, tm):
    m, d = x2.shape
    return pl.pallas_call(
        functools.partial(_out_kernel, alpha=alpha),
        out_shape=jax.ShapeDtypeStruct((m, d), F32),
        grid=(m // tm,),
        in_specs=[pl.BlockSpec((tm, WIDTH_A), lambda i: (i, 0)),
                  pl.BlockSpec((tm, WIDTH_B), lambda i: (i, 0)),
                  pl.BlockSpec((tm, d), lambda i: (i, 0)),
                  pl.BlockSpec((WIDTH_A + WIDTH_B, d), lambda i: (0, 0)),
                  pl.BlockSpec((1, d), lambda i: (0, 0)),
                  pl.BlockSpec((1, d), lambda i: (0, 0))],
        out_specs=pl.BlockSpec((tm, d), lambda i: (i, 0)),
        compiler_params=pltpu.CompilerParams(
            dimension_semantics=("arbitrary",), vmem_limit_bytes=VMEM_LIMIT_BYTES),
        name="out_proj_ln",
    )(mix_a, mix_b, x2, w_out, ln_g, ln_b)


def _pack_w_in(w):
    scale = LOG2E / math.sqrt(HEAD_DIM)
    sizes = (WIDTH_A,) * 4 + (WIDTH_B,) * 4 + (IDX_HEADS * IDX_DIM, IDX_DIM, IDX_HEADS)
    qa, ka, va, ga, qb, kb, vb, gb, qi, ki, wi = jnp.split(w, np.cumsum(sizes)[:-1].tolist(), axis=1)
    w_row = jnp.concatenate([qa * scale, ka, qb * scale, kb, qi, ki, ki, ga, gb], axis=1)
    wi_pad = jnp.pad(wi, ((0, 0), (0, T_WI_ROWS - IDX_HEADS)))
    w_t = jnp.concatenate([va, vb, wi_pad], axis=1).T
    return w_row.astype(BF16), w_t.astype(BF16)


def _layer(x, w_in, w_out, ln_g, ln_b, rel_bias, alpha, topk):
    b, s, d = x.shape
    tq = min(256, s)
    tm = min(512, b * s)
    assert s % tq == 0 and (b * s) % tm == 0 and tm % tq == 0 and topk <= tq
    x2 = x.reshape(b * s, d)
    w_row, w_t = _pack_w_in(w_in)
    p2, g2, vt, wi_t = _proj(x2, w_row, w_t, tm=tm, tk=tq)
    p3 = p2.reshape(b, s, P_COLS)
    g3 = g2.reshape(b, s, G_COLS)
    mix_a = _sb_attention(p3, g3, vt, tq)
    mix_b = _dsa_attention(p3, g3, vt, wi_t, rel_bias, tq, topk)
    out = _out_proj_ln(mix_a.reshape(b * s, WIDTH_A), mix_b.reshape(b * s, WIDTH_B), x2,
                       w_out.astype(BF16), ln_g.reshape(1, d), ln_b.reshape(1, d), alpha, tm=tm)
    return out.reshape(b, s, d)


def kernel(x, w_in, w_out, ln_g, ln_b, rel_bias):
    depth = w_in.shape[0]
    alpha = (2.0 * depth) ** 0.25
    topk = min(TOPK_MAX, x.shape[1] // 4)
    h = x
    for layer in range(depth):
        h = _layer(h, w_in[layer], w_out[layer], ln_g[layer], ln_b[layer], rel_bias, alpha, topk)
    return h
```

```python
import functools
import math

import jax
import jax.numpy as jnp
import numpy as np
from jax import lax
from jax.experimental import pallas as pl
from jax.experimental.pallas import tpu as pltpu

HEAD_DIM = 64
HEADS_A = 8
HEADS_B = 8
WIDTH_A = HEADS_A * HEAD_DIM
WIDTH_B = HEADS_B * HEAD_DIM
IDX_HEADS = 8
IDX_DIM = 64
CHUNK = 64
TOPK_MAX = 256
N_BUCKETS = 32
MAX_DISTANCE = 128
LN_EPS = 1e-5

LANES = 128
SUBLANES = 8
HEADS_PER_TILE = LANES // HEAD_DIM
VMEM_LIMIT_BYTES = 56 * 1024 * 1024

P_QA, P_KA, P_QB, P_KB, P_QI = (i * 512 for i in range(5))
P_KI = 5 * 512
P_COLS = P_KI + LANES
G_GA, G_GB = 0, 512
G_COLS = 1024
T_VA, T_VB, T_WI = 0, 512, 1024
T_WI_ROWS = 16
T_ROWS = T_WI + T_WI_ROWS

NEG_MASK = -1e30
MOST_NEGATIVE = float(np.finfo(np.float32).min)
NEG_INF_KEY16 = -32641
NO_CUT = 2 ** 30
NO_INDEX = 2 ** 15 - 1
HALF16 = 2 ** 15
SOFTMAX_SLACK = 40.0
LOG2E = math.log2(math.e)
DENOM_ROWS = 16
P1_GROUP = 4
FAR_GROUP = 4
SB_HEADS_PER_STEP = 8
LOG2_DEAD = -151.0

BF16 = jnp.bfloat16
F32 = jnp.float32


def _dot_nt(a, b):
    return lax.dot_general(a, b, (((1,), (1,)), ((), ())), preferred_element_type=F32)


def _dot(a, b):
    return jnp.dot(a, b, preferred_element_type=F32)


def _mask_head(x, parity):
    lane = lax.broadcasted_iota(jnp.int32, x.shape, 1)
    keep = (lane < HEAD_DIM) if parity == 0 else (lane >= HEAD_DIM)
    return jnp.where(keep, x, jnp.zeros_like(x))


def _tree_sum(xs):
    while len(xs) > 1:
        xs = [a + b for a, b in zip(xs[::2], xs[1::2])] + ([xs[-1]] if len(xs) % 2 else [])
    return xs[0]


def _silu(g):
    return g / (1.0 + jnp.exp(-g))


def _proj_kernel(x_ref, wr_ref, wt_ref, p_ref, g_ref, vt_ref, wi_ref, *, tk):
    xb = x_ref[...].astype(BF16)
    for c in range(0, P_COLS, 512):
        cw = min(512, P_COLS - c)
        p_ref[:, c:c + cw] = _dot(xb, wr_ref[:, c:c + cw]).astype(BF16)
    for c in range(0, G_COLS, 512):
        g_ref[:, c:c + 512] = _dot(xb, wr_ref[:, P_COLS + c:P_COLS + c + 512])
    for r in range(0, T_WI, 512):
        t = _dot_nt(wt_ref[r:r + 512, :], xb).astype(BF16)
        for kb in range(xb.shape[0] // tk):
            vt_ref[kb, r:r + 512, :] = t[:, kb * tk:(kb + 1) * tk]
    wi_ref[...] = _dot_nt(wt_ref[T_WI:, :], xb)


def _proj(x2, w_row, w_t, tm, tk):
    m, d = x2.shape
    return pl.pallas_call(
        functools.partial(_proj_kernel, tk=tk),
        out_shape=(jax.ShapeDtypeStruct((m, P_COLS), BF16),
                   jax.ShapeDtypeStruct((m, G_COLS), F32),
                   jax.ShapeDtypeStruct((m // tk, T_WI, tk), BF16),
                   jax.ShapeDtypeStruct((T_WI_ROWS, m), F32)),
        grid=(m // tm,),
        in_specs=[pl.BlockSpec((tm, d), lambda i: (i, 0)),
                  pl.BlockSpec((d, P_COLS + G_COLS), lambda i: (0, 0)),
                  pl.BlockSpec((T_ROWS, d), lambda i: (0, 0))],
        out_specs=(pl.BlockSpec((tm, P_COLS), lambda i: (i, 0)),
                   pl.BlockSpec((tm, G_COLS), lambda i: (i, 0)),
                   pl.BlockSpec((tm // tk, T_WI, tk), lambda i: (i, 0, 0)),
                   pl.BlockSpec((T_WI_ROWS, tm), lambda i: (0, i))),
        compiler_params=pltpu.CompilerParams(
            dimension_semantics=("arbitrary",), vmem_limit_bytes=VMEM_LIMIT_BYTES),
        name="proj",
    )(x2, w_row, w_t)


def _softplus2(z):
    return jnp.maximum(z, 0.0) + jnp.log2(1.0 + jnp.exp2(-jnp.abs(z)))


def _sb_blocks(chains, u2):
    half = u2.shape[0]
    stricts = [c[3] for c in chains]
    masked = lambda xs: [x if m is None else jnp.where(m, x, 0.0) for x, m in zip(xs, stricts)]
    zts = [_dot_nt(kblk, qh) for qh, kblk, _, _ in chains]
    sps = masked([_softplus2(zt) for zt in zts])
    his = [sp.astype(BF16) for sp in sps]
    los = [(sp - hi.astype(F32)).astype(BF16) for sp, hi in zip(sps, his)]
    halves = lambda x: (x[:half], x[half:])
    tails = []
    for sp, hi, lo in zip(sps, his, los):
        (hi_a, hi_b), (lo_a, lo_b) = halves(hi), halves(lo)
        tail_a = _dot(u2, jnp.concatenate([hi_a, lo_a], axis=0))
        tail_b = _dot(u2, jnp.concatenate([hi_b, lo_b], axis=0))
        total_b = tail_b[:1, :] - sp[half:half + 1, :]
        tails.append(jnp.concatenate([tail_a + total_b, tail_b], axis=0))
    ws = masked([jnp.exp2(zt - sp + tail) for zt, sp, tail in zip(zts, sps, tails)])
    pvs = [_dot(c[2], a.astype(BF16)) for c, a in zip(chains, ws)]
    totals = [tail[:1, :] - sp[:1, :] for tail, sp in zip(tails, sps)]
    return list(zip(pvs, totals))


def _sb_kernel(q_ref, k_ref, vt_ref, u_ref, g_ref, o_ref, *, tq, n_heads):
    qi = pl.program_id(2)
    pair = lambda h: slice((h // HEADS_PER_TILE) * LANES, (h // HEADS_PER_TILE + 1) * LANES)
    qh = [_mask_head(q_ref[:, pair(h)], h % HEADS_PER_TILE) for h in range(n_heads)]
    u2 = u_ref[...]
    key_i = lax.broadcasted_iota(jnp.int32, (tq, tq), 0)
    qry_i = lax.broadcasted_iota(jnp.int32, (tq, tq), 1)
    strict = key_i < qry_i

    def chains(j, mask):
        start = pl.multiple_of(j * tq, tq)
        return [(qh[h], k_ref[pl.ds(start, tq), pair(h)], vt_ref[j, pair(h), :], mask)
                for h in range(n_heads)]

    has_prev = qi > 0
    res = _sb_blocks(chains(qi, strict) + chains(jnp.maximum(qi - 1, 0), None), u2)
    state = []
    for h in range(n_heads):
        (pv_d, tot_d), (pv_p, tot_p) = res[h], res[n_heads + h]
        state += [pv_d + jnp.where(has_prev, jnp.exp2(tot_d), 0.0) * pv_p, tot_d + tot_p]

    def alive_after(st):
        top = functools.reduce(jnp.maximum, st[1::2])
        return jnp.max(top) > LOG2_DEAD

    def cond(carry):
        j, alive = carry[0], carry[1]
        return jnp.logical_and(j >= 0, alive)

    def body(carry):
        j, st = carry[0], carry[2:]
        res = _sb_blocks(chains(j, None), u2)
        out = []
        for h in range(n_heads):
            acc, run = st[2 * h], st[2 * h + 1]
            pv, total = res[h]
            out += [acc + jnp.exp2(run) * pv, run + total]
        return (j - 1, alive_after(out)) + tuple(out)

    fin = lax.while_loop(cond, body, (qi - 2, alive_after(state)) + tuple(state))[2:]
    row = lax.broadcasted_iota(jnp.int32, fin[0].shape, 0)
    for hp in range(n_heads // HEADS_PER_TILE):
        acc0, acc1 = fin[2 * (2 * hp)], fin[2 * (2 * hp + 1)]
        ot = jnp.where(row < HEAD_DIM, acc0, acc1)
        o_ref[:, pair(2 * hp)] = (ot.T * _silu(g_ref[:, pair(2 * hp)])).astype(o_ref.dtype)


def _sb_attention(p3, g3, vt, tq):
    b, s, _ = p3.shape
    nkb = s // tq
    half = tq // 2
    tri = np.triu(-np.ones((half, half), np.float32), 1)
    u2 = jnp.asarray(np.concatenate([tri, tri], axis=1), BF16)
    w = SB_HEADS_PER_STEP * HEAD_DIM
    return pl.pallas_call(
        functools.partial(_sb_kernel, tq=tq, n_heads=SB_HEADS_PER_STEP),
        out_shape=jax.ShapeDtypeStruct((b, s, WIDTH_A), BF16),
        grid=(b, HEADS_A // SB_HEADS_PER_STEP, nkb),
        in_specs=[
            pl.BlockSpec((None, tq, w), lambda bi, hp, qi: (bi, qi, P_QA // w + hp)),
            pl.BlockSpec((None, s, w), lambda bi, hp, qi: (bi, 0, P_KA // w + hp)),
            pl.BlockSpec((nkb, w, tq), lambda bi, hp, qi: (bi, T_VA // w + hp, 0)),
            pl.BlockSpec((half, 2 * half), lambda bi, hp, qi: (0, 0)),
            pl.BlockSpec((None, tq, w), lambda bi, hp, qi: (bi, qi, G_GA // w + hp)),
        ],
        out_specs=pl.BlockSpec((None, tq, w), lambda bi, hp, qi: (bi, qi, hp)),
        compiler_params=pltpu.CompilerParams(
            dimension_semantics=("arbitrary", "arbitrary", "arbitrary"),
            vmem_limit_bytes=VMEM_LIMIT_BYTES),
        name="sb_attention",
    )(p3, p3, vt, u2, g3)


def _dsa_kernel(rb_ref, far_ref, qi_ref, wi_ref, ki_ref, q_ref, k_ref, vt_ref, bkt_ref, g_ref,
                o_ref, sc_ref, hi_ref, eq_ref, bias_ref, m_ref, top_ref, acc_ref, *, tq, topk, seq):
    qi = pl.program_id(1)
    nblk = qi + 1

    @pl.when(jnp.logical_and(pl.program_id(0) == 0, qi == 0))
    def _():
        far = far_ref[0]
        for h in range(HEADS_B):
            for d in range(2):
                bkt = bkt_ref[d]

                def bucket_body(b, tile, bkt=bkt, h=h):
                    return jnp.where(bkt == b, (rb_ref[b, h] - rb_ref[far, h]) * LOG2E, tile)

                bias_ref[h, d] = lax.fori_loop(0, N_BUCKETS, bucket_body, jnp.zeros((tq, tq), F32))

    wi = wi_ref[...]
    qih = [_mask_head(qi_ref[:, (h // 2) * LANES:(h // 2 + 1) * LANES], h % 2)
           for h in range(IDX_HEADS)]

    def score_block(j, nb=1):
        start = pl.multiple_of(j * tq, tq)
        kib = ki_ref[pl.ds(start, tq * nb), :]
        rels = [_dot_nt(kib, qih[h]) for h in range(IDX_HEADS)]
        sc = jnp.zeros((tq * nb, tq), F32)
        for h in range(IDX_HEADS):
            sc = sc + jnp.maximum(rels[h], 0.0) * wi[h:h + 1, :]
        return sc

    def store_scores(j, sc):
        sc_ref[j] = sc
        hi_ref[j] = sc.astype(BF16)

    def score_group(j, nb):
        sc = score_block(j, nb)
        for i in range(nb):
            store_scores(j + i, sc[i * tq:(i + 1) * tq])

    def p1_body(i, c):
        score_group(P1_GROUP * i, P1_GROUP)
        return c

    n_groups1 = qi // P1_GROUP
    lax.fori_loop(0, n_groups1, p1_body, 0)
    done1 = n_groups1 * P1_GROUP
    size = P1_GROUP // 2
    while size >= 1:
        take = (qi // size) % 2 == 1

        @pl.when(take)
        def _(done1=done1, size=size):
            score_group(done1, size)

        done1 = done1 + jnp.where(take, size, 0)
        size //= 2

    key_i = lax.broadcasted_iota(jnp.int32, (tq, tq), 0)
    qry_i = lax.broadcasted_iota(jnp.int32, (tq, tq), 1)
    admissible = key_i < (qry_i // CHUNK + 1) * CHUNK
    store_scores(qi, jnp.where(admissible, score_block(qi), -jnp.inf))

    def count(pred):
        def body(i, acc):
            for j in (2 * i, 2 * i + 1):
                ones = jnp.where(pred(sc_ref[j], j * tq), 1.0, 0.0)
                acc = acc + _tree_sum([ones[r:r + SUBLANES, :] for r in range(0, tq, SUBLANES)])
            return acc
        acc = lax.fori_loop(0, (nblk + 1) // 2, body, jnp.zeros((SUBLANES, tq), F32))
        return jnp.sum(acc, axis=0, keepdims=True)

    def count16(ref, pred):
        rows = 2 * SUBLANES
        def body(i, acc):
            for blk in (ref[2 * i], ref[2 * i + 1]):
                ones = jnp.where(pred(blk), jnp.int16(1), jnp.int16(0))
                acc = acc + _tree_sum([ones[r:r + rows, :] for r in range(0, tq, rows)])
            return acc
        acc = lax.fori_loop(0, (nblk + 1) // 2, body, jnp.zeros((rows, tq), jnp.int16))
        return jnp.sum(acc.astype(F32), axis=0, keepdims=True)

    def count_hi(cand):
        return count16(hi_ref, lambda blk: blk >= cand)

    @pl.when(nblk % 2 == 1)
    def _():
        sc_ref[nblk] = jnp.full((tq, tq), -jnp.inf, F32)
        hi_ref[nblk] = jnp.full((tq, tq), -jnp.inf, BF16)

    def float_of_key(key):
        return lax.bitcast_convert_type(jnp.where(key < 0, key ^ 0x7FFFFFFF, key), F32)

    def key32_of_key16(key16):
        return key16 * (2 * HALF16) + jnp.where(key16 < 0, 2 * HALF16 - 1, 0)

    scanned = (nblk * tq).astype(F32)
    kf = float(topk)
    assert seq // (2 * SUBLANES) < 2 ** 15

    def hi_body(i, prefix):
        cand_u = prefix | jnp.left_shift(jnp.int32(1), 15 - i)
        key16 = cand_u - HALF16
        cand = float_of_key(key32_of_key16(key16)).astype(BF16)
        cnt = jnp.where(key16 <= NEG_INF_KEY16, scanned, count_hi(cand))
        return jnp.where(cnt >= kf, cand_u, prefix)

    hi_key = lax.fori_loop(0, 16, hi_body, jnp.zeros((1, tq), jnp.int32)) - HALF16
    finite_thr = hi_key > NEG_INF_KEY16
    base = jnp.where(finite_thr, key32_of_key16(hi_key) - HALF16, 0)
    base_val = float_of_key(base)
    base_cnt = count(lambda blk, k0: blk >= base_val)

    def lo_body(i, carry):
        off, cnt_ge, cnt_gt = carry
        off_try = off | jnp.left_shift(jnp.int32(1), 16 - i)
        cand = float_of_key(base + off_try)
        cnt = count(lambda blk, k0: blk >= cand)
        take = cnt >= kf
        return jnp.where(take, off_try, off), jnp.where(take, cnt, cnt_ge), jnp.where(take, cnt_gt, cnt)

    off, cnt_ge, cnt_gt = lax.fori_loop(
        0, 17, lo_body, (jnp.zeros((1, tq), jnp.int32), base_cnt, jnp.zeros((1, tq), F32)))
    thr = float_of_key(base + off)

    tie = jnp.logical_and(cnt_ge > kf, finite_thr)
    has_tie = jnp.max(jnp.where(tie, 1.0, 0.0)) > 0.0

    @pl.when(jnp.logical_not(has_tie))
    def _():
        thr_min = jnp.where(finite_thr, thr, MOST_NEGATIVE)

        def mask_body(j, c):
            sc_ref[j] = jnp.where(sc_ref[j] >= thr_min, 0.0, NEG_MASK)
            return c

        lax.fori_loop(0, nblk, mask_body, 0)

    @pl.when(has_tie)
    def _():
        need = kf - cnt_gt
        nbits = int(seq - 1).bit_length()
        assert seq < NO_INDEX

        def eq_body(j, c):
            idx = jnp.where(sc_ref[j] == thr, key_i + j * tq, NO_INDEX)
            eq_ref[j] = idx.astype(jnp.int16)
            return c

        lax.fori_loop(0, nblk, eq_body, 0)

        @pl.when(nblk % 2 == 1)
        def _():
            eq_ref[nblk] = jnp.full((tq, tq), NO_INDEX, jnp.int16)

        def idx_body(i, lo):
            c_try = lo + jnp.left_shift(jnp.int32(1), nbits - 1 - i)
            c16 = c_try.astype(jnp.int16)
            f = count16(eq_ref, lambda blk: blk < c16)
            return jnp.where(f < need, c_try, lo)

        lo = lax.fori_loop(0, nbits, idx_body, jnp.zeros((1, tq), jnp.int32))
        cut = jnp.where(tie, lo + 1, jnp.where(finite_thr, NO_CUT, 0))
        thr_gt = jnp.where(finite_thr, thr, -jnp.inf)

        def mask_body(j, c):
            blk = sc_ref[j]
            at_thr = jnp.where((key_i + j * tq) < cut, 0.0, NEG_MASK)
            sc_ref[j] = jnp.where(blk > thr_gt, 0.0, jnp.where(blk == thr_gt, at_thr, NEG_MASK))
            return c

        lax.fori_loop(0, nblk, mask_body, 0)

    qh = [_mask_head(q_ref[:, (h // 2) * LANES:(h // 2 + 1) * LANES], h % 2)
          for h in range(HEADS_B)]
    m_ref[...] = jnp.full(m_ref.shape, NEG_MASK, F32)
    acc_ref[...] = jnp.zeros(acc_ref.shape, F32)

    def att_block(j, nb, d, stream=False):
        start = pl.multiple_of(j * tq, tq)
        cat = lambda xs, axis: xs[0] if nb == 1 else jnp.concatenate(xs, axis=axis)
        mask = cat([sc_ref[j + i] for i in range(nb)], 0)
        ones = jnp.ones((DENOM_ROWS, tq * nb), BF16)
        vts = [jnp.concatenate(
            [cat([vt_ref[j + i, h * HEAD_DIM:(h + 1) * HEAD_DIM, :] for i in range(nb)], 1), ones], axis=0)
            for h in range(HEADS_B)]
        pair = lambda h: slice((h // HEADS_PER_TILE) * LANES, (h // HEADS_PER_TILE + 1) * LANES)
        qk = lambda h: _dot_nt(k_ref[pl.ds(start, tq * nb), pair(h)], qh[h])
        m_old = m_ref[...]
        hs = range(HEADS_B)

        def biased_logits(group):
            lgs = [qk(h) + mask for h in group]
            if d is not None:
                lgs = [lg + bias_ref[h, d] for lg, h in zip(lgs, group)]
            return lgs

        if stream:
            logits = biased_logits(hs)
            ps = [jnp.exp2(logits[h] - m_old[h:h + 1, :]).astype(BF16) for h in hs]
            bmax = jnp.concatenate([jnp.max(logits[h], axis=0, keepdims=True) for h in hs], axis=0)
            pvs = [_dot(vts[h], ps[h]) for h in hs]
            top_old = top_ref[...]
            accs = [acc_ref[h] + pvs[h] for h in hs]
            top_ref[...] = jnp.maximum(top_old, bmax)
            for h in hs:
                acc_ref[h] = accs[h]
            return
        logits = biased_logits(hs)
        m_new = [jnp.maximum(m_old[h:h + 1, :], jnp.max(logits[h], axis=0, keepdims=True)) for h in hs]
        ps = [jnp.exp2(logits[h] - m_new[h]).astype(BF16) for h in hs]
        pvs = [_dot(vts[h], ps[h]) for h in hs]
        accs = [jnp.exp2(m_old[h:h + 1, :] - m_new[h]) * acc_ref[h] + pvs[h] for h in hs]
        m_ref[...] = jnp.concatenate(m_new, axis=0)
        for h in range(HEADS_B):
            acc_ref[h] = accs[h]

    n_far = qi - 1

    @pl.when(qi > 0)
    def _():
        att_block(qi - 1, 1, 1)
        top_ref[...] = m_ref[...]
        att_block(qi, 1, 0, stream=True)

    @pl.when(qi == 0)
    def _():
        att_block(qi, 1, 0)
        top_ref[...] = m_ref[...]

    def far_body(i, c):
        att_block(FAR_GROUP * i, FAR_GROUP, None, stream=True)
        return c

    n_groups = jnp.maximum(n_far, 0) // FAR_GROUP
    lax.fori_loop(0, n_groups, far_body, 0)
    done = n_groups * FAR_GROUP
    size = FAR_GROUP // 2
    while size >= 1:
        take = jnp.logical_and(n_far > 0, (n_far // size) % 2 == 1)

        @pl.when(take)
        def _(done=done, size=size):
            att_block(done, size, None, stream=True)

        done = done + jnp.where(take, size, 0)
        size //= 2

    @pl.when(jnp.max(top_ref[...] - m_ref[...]) > SOFTMAX_SLACK)
    def _():
        m_ref[...] = jnp.full(m_ref.shape, NEG_MASK, F32)
        acc_ref[...] = jnp.zeros(acc_ref.shape, F32)

        def exact_far(j, c):
            att_block(j, 1, None)
            return c

        lax.fori_loop(0, n_far, exact_far, 0)

        @pl.when(qi > 0)
        def _():
            att_block(qi - 1, 1, 1)

        att_block(qi, 1, 0)

    for hp in range(HEADS_B // HEADS_PER_TILE):
        h0 = hp * HEADS_PER_TILE
        a0, a1 = acc_ref[h0], acc_ref[h0 + 1]
        ot = jnp.concatenate([a0[:HEAD_DIM] / a0[HEAD_DIM:HEAD_DIM + 1, :],
                              a1[:HEAD_DIM] / a1[HEAD_DIM:HEAD_DIM + 1, :]], axis=0)
        gp = g_ref[:, hp * LANES:(hp + 1) * LANES]
        o_ref[:, hp * LANES:(hp + 1) * LANES] = (ot.T * _silu(gp)).astype(o_ref.dtype)


def _t5_bucket(rel):
    half = N_BUCKETS // 2
    max_exact = half // 2
    ret = jnp.where(rel > 0, half, 0)
    n = jnp.abs(rel)
    nf = jnp.maximum(n, 1).astype(jnp.float32)
    large = max_exact + (jnp.log(nf / max_exact) / math.log(MAX_DISTANCE / max_exact)
                         * (half - max_exact)).astype(jnp.int32)
    large = jnp.minimum(large, half - 1)
    return ret + jnp.where(n < max_exact, n, large)


def _dsa_attention(p3, g3, vt, wi_t, rel_bias, tq, topk):
    b, s, _ = p3.shape
    nkb = s // tq
    assert tq >= MAX_DISTANCE
    key = jnp.arange(tq, dtype=jnp.int32)[:, None]
    qry = jnp.arange(tq, dtype=jnp.int32)[None, :]
    buckets = jnp.stack([_t5_bucket(key - qry), _t5_bucket(key - qry - tq)])
    far = _t5_bucket(jnp.full((1,), -tq - 1, jnp.int32))
    buckets = jnp.bitwise_and(buckets, 2 * N_BUCKETS - 1)
    far = jnp.bitwise_and(far, 2 * N_BUCKETS - 1)
    row = lambda blk: (lambda bi, qi: (bi, qi, blk))
    full = lambda blk: (lambda bi, qi: (bi, 0, blk))
    smem = pl.BlockSpec(memory_space=pltpu.SMEM)
    return pl.pallas_call(
        functools.partial(_dsa_kernel, tq=tq, topk=topk, seq=s),
        out_shape=jax.ShapeDtypeStruct((b, s, WIDTH_B), BF16),
        grid=(b, nkb),
        in_specs=[
            smem, smem,
            pl.BlockSpec((None, tq, 512), row(P_QI // 512)),
            pl.BlockSpec((T_WI_ROWS, tq), lambda bi, qi: (0, bi * nkb + qi)),
            pl.BlockSpec((None, s, LANES), full(P_KI // LANES)),
            pl.BlockSpec((None, tq, 512), row(P_QB // 512)),
            pl.BlockSpec((None, s, 512), full(P_KB // 512)),
            pl.BlockSpec((nkb, WIDTH_B, tq), lambda bi, qi: (bi, T_VB // WIDTH_B, 0)),
            pl.BlockSpec((2, tq, tq), lambda bi, qi: (0, 0, 0)),
            pl.BlockSpec((None, tq, 512), row(G_GB // 512)),
        ],
        out_specs=pl.BlockSpec((None, tq, WIDTH_B), lambda bi, qi: (bi, qi, 0)),
        scratch_shapes=[pltpu.VMEM((nkb + nkb % 2, tq, tq), F32),
                        pltpu.VMEM((nkb + nkb % 2, tq, tq), BF16),
                        pltpu.VMEM((nkb + nkb % 2, tq, tq), jnp.int16),
                        pltpu.VMEM((HEADS_B, 2, tq, tq), F32),
                        pltpu.VMEM((HEADS_B, tq), F32),
                        pltpu.VMEM((HEADS_B, tq), F32),
                        pltpu.VMEM((HEADS_B, HEAD_DIM + DENOM_ROWS, tq), F32)],
        compiler_params=pltpu.CompilerParams(
            dimension_semantics=("arbitrary", "arbitrary"),
            vmem_limit_bytes=VMEM_LIMIT_BYTES),
        name="dsa_attention",
    )(rel_bias, far, p3, wi_t, p3, p3, p3, vt, buckets, g3)


def _out_kernel(ma_ref, mb_ref, x_ref, w_ref, lg_ref, lb_ref, o_ref, *, alpha):
    y = _dot(ma_ref[...], w_ref[:WIDTH_A, :]) + _dot(mb_ref[...], w_ref[WIDTH_A:, :])
    h = alpha * x_ref[...] + y
    mu = jnp.mean(h, axis=-1, keepdims=True)
    hc = h - mu
    var = jnp.mean(hc * hc, axis=-1, keepdims=True)
    o_ref[...] = hc * lax.rsqrt(var + LN_EPS) * lg_ref[...] + lb_ref[...]


def _out_proj_ln(mix_a, mix_b, x2, w_out, ln_g, ln_b, alpha, tm):
    m, d = x2.shape
    return pl.pallas_call(
        functools.partial(_out_kernel, alpha=alpha),
        out_shape=jax.ShapeDtypeStruct((m, d), F32),
        grid=(m // tm,),
        in_specs=[pl.BlockSpec((tm, WIDTH_A), lambda i: (i, 0)),
                  pl.BlockSpec((tm, WIDTH_B), lambda i: (i, 0)),
                  pl.BlockSpec((tm, d), lambda i: (i, 0)),
                  pl.BlockSpec((WIDTH_A + WIDTH_B, d), lambda i: (0, 0)),
                  pl.BlockSpec((1, d), lambda i: (0, 0)),
                  pl.BlockSpec((1, d), lambda i: (0, 0))],
        out_specs=pl.BlockSpec((tm, d), lambda i: (i, 0)),
        compiler_params=pltpu.CompilerParams(
            dimension_semantics=("arbitrary",), vmem_limit_bytes=VMEM_LIMIT_BYTES),
        name="out_proj_ln",
    )(mix_a, mix_b, x2, w_out, ln_g, ln_b)


def _pack_w_in(w):
    scale = LOG2E / math.sqrt(HEAD_DIM)
    sizes = (WIDTH_A,) * 4 + (WIDTH_B,) * 4 + (IDX_HEADS * IDX_DIM, IDX_DIM, IDX_HEADS)
    qa, ka, va, ga, qb, kb, vb, gb, qi, ki, wi = jnp.split(w, np.cumsum(sizes)[:-1].tolist(), axis=1)
    w_row = jnp.concatenate([qa * scale, ka, qb * scale, kb, qi, ki, ki, ga, gb], axis=1)
    wi_pad = jnp.pad(wi, ((0, 0), (0, T_WI_ROWS - IDX_HEADS)))
    w_t = jnp.concatenate([va, vb, wi_pad], axis=1).T
    return w_row.astype(BF16), w_t.astype(BF16)


def _layer(x, w_in, w_out, ln_g, ln_b, rel_bias, alpha, topk):
    b, s, d = x.shape
    tq = min(256, s)
    tm = min(512, b * s)
    assert s % tq == 0 and (b * s) % tm == 0 and tm % tq == 0 and topk <= tq
    x2 = x.reshape(b * s, d)
    w_row, w_t = _pack_w_in(w_in)
    p2, g2, vt, wi_t = _proj(x2, w_row, w_t, tm=tm, tk=tq)
    p3 = p2.reshape(b, s, P_COLS)
    g3 = g2.reshape(b, s, G_COLS)
    mix_a = _sb_attention(p3, g3, vt, tq)
    mix_b = _dsa_attention(p3, g3, vt, wi_t, rel_bias, tq, topk)
    out = _out_proj_ln(mix_a.reshape(b * s, WIDTH_A), mix_b.reshape(b * s, WIDTH_B), x2,
                       w_out.astype(BF16), ln_g.reshape(1, d), ln_b.reshape(1, d), alpha, tm=tm)
    return out.reshape(b, s, d)


def kernel(x, w_in, w_out, ln_g, ln_b, rel_bias):
    depth = w_in.shape[0]
    alpha = (2.0 * depth) ** 0.25
    topk = min(TOPK_MAX, x.shape[1] // 4)
    h = x
    for layer in range(depth):
        h = _layer(h, w_in[layer], w_out[layer], ln_g[layer], ln_b[layer], rel_bias, alpha, topk)
    return h
```

```python
import functools
import math

import jax
import jax.numpy as jnp
import numpy as np
from jax import lax
from jax.experimental import pallas as pl
from jax.experimental.pallas import tpu as pltpu

HEAD_DIM = 64
HEADS_A = 8
HEADS_B = 8
WIDTH_A = HEADS_A * HEAD_DIM
WIDTH_B = HEADS_B * HEAD_DIM
IDX_HEADS = 8
IDX_DIM = 64
CHUNK = 64
TOPK_MAX = 256
N_BUCKETS = 32
MAX_DISTANCE = 128
LN_EPS = 1e-5

LANES = 128
SUBLANES = 8
HEADS_PER_TILE = LANES // HEAD_DIM
VMEM_LIMIT_BYTES = 56 * 1024 * 1024

P_QA, P_KA, P_QB, P_KB, P_QI = (i * 512 for i in range(5))
P_KI = 5 * 512
P_COLS = P_KI + LANES
G_GA, G_GB = 0, 512
G_COLS = 1024
T_VA, T_VB, T_WI = 0, 512, 1024
T_WI_ROWS = 16
T_ROWS = T_WI + T_WI_ROWS

NEG_MASK = -1e30
MOST_NEGATIVE = float(np.finfo(np.float32).min)
NEG_INF_KEY16 = -32641
NO_CUT = 2 ** 30
NO_INDEX = 2 ** 15 - 1
HALF16 = 2 ** 15
SOFTMAX_SLACK = 40.0
LOG2E = math.log2(math.e)
DENOM_ROWS = 16
P1_GROUP = 8
FAR_GROUP = 4
SB_HEADS_PER_STEP = 8
LOG2_DEAD = -151.0

BF16 = jnp.bfloat16
F32 = jnp.float32


def _dot_nt(a, b):
    return lax.dot_general(a, b, (((1,), (1,)), ((), ())), preferred_element_type=F32)


def _dot(a, b):
    return jnp.dot(a, b, preferred_element_type=F32)


def _mask_head(x, parity):
    lane = lax.broadcasted_iota(jnp.int32, x.shape, 1)
    keep = (lane < HEAD_DIM) if parity == 0 else (lane >= HEAD_DIM)
    return jnp.where(keep, x, jnp.zeros_like(x))


def _tree_sum(xs):
    while len(xs) > 1:
        xs = [a + b for a, b in zip(xs[::2], xs[1::2])] + ([xs[-1]] if len(xs) % 2 else [])
    return xs[0]


def _silu(g):
    return g / (1.0 + jnp.exp(-g))


def _proj_kernel(x_ref, wr_ref, wt_ref, p_ref, g_ref, vt_ref, wi_ref, *, tk):
    xb = x_ref[...].astype(BF16)
    for c in range(0, P_COLS, 512):
        cw = min(512, P_COLS - c)
        p_ref[:, c:c + cw] = _dot(xb, wr_ref[:, c:c + cw]).astype(BF16)
    for c in range(0, G_COLS, 512):
        g_ref[:, c:c + 512] = _dot(xb, wr_ref[:, P_COLS + c:P_COLS + c + 512])
    for r in range(0, T_WI, 512):
        t = _dot_nt(wt_ref[r:r + 512, :], xb).astype(BF16)
        for kb in range(xb.shape[0] // tk):
            vt_ref[kb, r:r + 512, :] = t[:, kb * tk:(kb + 1) * tk]
    wi_ref[...] = _dot_nt(wt_ref[T_WI:, :], xb)


def _proj(x2, w_row, w_t, tm, tk):
    m, d = x2.shape
    return pl.pallas_call(
        functools.partial(_proj_kernel, tk=tk),
        out_shape=(jax.ShapeDtypeStruct((m, P_COLS), BF16),
                   jax.ShapeDtypeStruct((m, G_COLS), F32),
                   jax.ShapeDtypeStruct((m // tk, T_WI, tk), BF16),
                   jax.ShapeDtypeStruct((T_WI_ROWS, m), F32)),
        grid=(m // tm,),
        in_specs=[pl.BlockSpec((tm, d), lambda i: (i, 0)),
                  pl.BlockSpec((d, P_COLS + G_COLS), lambda i: (0, 0)),
                  pl.BlockSpec((T_ROWS, d), lambda i: (0, 0))],
        out_specs=(pl.BlockSpec((tm, P_COLS), lambda i: (i, 0)),
                   pl.BlockSpec((tm, G_COLS), lambda i: (i, 0)),
                   pl.BlockSpec((tm // tk, T_WI, tk), lambda i: (i, 0, 0)),
                   pl.BlockSpec((T_WI_ROWS, tm), lambda i: (0, i))),
        compiler_params=pltpu.CompilerParams(
            dimension_semantics=("arbitrary",), vmem_limit_bytes=VMEM_LIMIT_BYTES),
        name="proj",
    )(x2, w_row, w_t)


def _softplus2(z):
    return jnp.maximum(z, 0.0) + jnp.log2(1.0 + jnp.exp2(-jnp.abs(z)))


def _sb_blocks(chains, u2):
    half = u2.shape[0]
    zts = [_dot_nt(kblk, qh) if m is None else jnp.where(m, _dot_nt(kblk, qh), NEG_MASK)
           for qh, kblk, _, m in chains]
    sps = [_softplus2(zt) for zt in zts]
    his = [sp.astype(BF16) for sp in sps]
    los = [(sp - hi.astype(F32)).astype(BF16) for sp, hi in zip(sps, his)]
    halves = lambda x: (x[:half], x[half:])
    tails = []
    for sp, hi, lo in zip(sps, his, los):
        (hi_a, hi_b), (lo_a, lo_b) = halves(hi), halves(lo)
        tail_a = _dot(u2, jnp.concatenate([hi_a, lo_a], axis=0))
        tail_b = _dot(u2, jnp.concatenate([hi_b, lo_b], axis=0))
        total_b = tail_b[:1, :] - sp[half:half + 1, :]
        tails.append(jnp.concatenate([tail_a + total_b, tail_b], axis=0))
    ws = [jnp.exp2(zt - sp + tail) for zt, sp, tail in zip(zts, sps, tails)]
    pvs = [_dot(c[2], a.astype(BF16)) for c, a in zip(chains, ws)]
    totals = [tail[:1, :] - sp[:1, :] for tail, sp in zip(tails, sps)]
    return list(zip(pvs, totals))


def _sb_kernel(q_ref, k_ref, vt_ref, u_ref, g_ref, o_ref, *, tq, n_heads):
    qi = pl.program_id(2)
    pair = lambda h: slice((h // HEADS_PER_TILE) * LANES, (h // HEADS_PER_TILE + 1) * LANES)
    qh = [_mask_head(q_ref[:, pair(h)], h % HEADS_PER_TILE) for h in range(n_heads)]
    u2 = u_ref[...]
    key_i = lax.broadcasted_iota(jnp.int32, (tq, tq), 0)
    qry_i = lax.broadcasted_iota(jnp.int32, (tq, tq), 1)
    strict = key_i < qry_i

    def chains(j, mask):
        start = pl.multiple_of(j * tq, tq)
        return [(qh[h], k_ref[pl.ds(start, tq), pair(h)], vt_ref[j, pair(h), :], mask)
                for h in range(n_heads)]

    has_prev = qi > 0
    res = _sb_blocks(chains(qi, strict) + chains(jnp.maximum(qi - 1, 0), None), u2)
    state = []
    for h in range(n_heads):
        (pv_d, tot_d), (pv_p, tot_p) = res[h], res[n_heads + h]
        state += [pv_d + jnp.where(has_prev, jnp.exp2(tot_d), 0.0) * pv_p, tot_d + tot_p]

    def alive_after(st):
        top = functools.reduce(jnp.maximum, st[1::2])
        return jnp.max(top) > LOG2_DEAD

    def cond(carry):
        j, alive = carry[0], carry[1]
        return jnp.logical_and(j >= 0, alive)

    def body(carry):
        j, st = carry[0], carry[2:]
        res = _sb_blocks(chains(j, None), u2)
        out = []
        for h in range(n_heads):
            acc, run = st[2 * h], st[2 * h + 1]
            pv, total = res[h]
            out += [acc + jnp.exp2(run) * pv, run + total]
        return (j - 1, alive_after(out)) + tuple(out)

    fin = lax.while_loop(cond, body, (qi - 2, alive_after(state)) + tuple(state))[2:]
    row = lax.broadcasted_iota(jnp.int32, fin[0].shape, 0)
    for hp in range(n_heads // HEADS_PER_TILE):
        acc0, acc1 = fin[2 * (2 * hp)], fin[2 * (2 * hp + 1)]
        ot = jnp.where(row < HEAD_DIM, acc0, acc1)
        o_ref[:, pair(2 * hp)] = (ot.T * _silu(g_ref[:, pair(2 * hp)])).astype(o_ref.dtype)


def _sb_attention(p3, g3, vt, tq):
    b, s, _ = p3.shape
    nkb = s // tq
    half = tq // 2
    tri = np.triu(-np.ones((half, half), np.float32), 1)
    u2 = jnp.asarray(np.concatenate([tri, tri], axis=1), BF16)
    w = SB_HEADS_PER_STEP * HEAD_DIM
    return pl.pallas_call(
        functools.partial(_sb_kernel, tq=tq, n_heads=SB_HEADS_PER_STEP),
        out_shape=jax.ShapeDtypeStruct((b, s, WIDTH_A), BF16),
        grid=(b, HEADS_A // SB_HEADS_PER_STEP, nkb),
        in_specs=[
            pl.BlockSpec((None, tq, w), lambda bi, hp, qi: (bi, qi, P_QA // w + hp)),
            pl.BlockSpec((None, s, w), lambda bi, hp, qi: (bi, 0, P_KA // w + hp)),
            pl.BlockSpec((nkb, w, tq), lambda bi, hp, qi: (bi, T_VA // w + hp, 0)),
            pl.BlockSpec((half, 2 * half), lambda bi, hp, qi: (0, 0)),
            pl.BlockSpec((None, tq, w), lambda bi, hp, qi: (bi, qi, G_GA // w + hp)),
        ],
        out_specs=pl.BlockSpec((None, tq, w), lambda bi, hp, qi: (bi, qi, hp)),
        compiler_params=pltpu.CompilerParams(
            dimension_semantics=("arbitrary", "arbitrary", "arbitrary"),
            vmem_limit_bytes=VMEM_LIMIT_BYTES),
        name="sb_attention",
    )(p3, p3, vt, u2, g3)


def _dsa_kernel(rb_ref, far_ref, qi_ref, wi_ref, ki_ref, q_ref, k_ref, vt_ref, bkt_ref, g_ref,
                o_ref, sc_ref, hi_ref, eq_ref, bias_ref, m_ref, top_ref, acc_ref, *, tq, topk, seq):
    qi = pl.program_id(1)
    nblk = qi + 1

    @pl.when(jnp.logical_and(pl.program_id(0) == 0, qi == 0))
    def _():
        far = far_ref[0]
        for h in range(HEADS_B):
            for d in range(2):
                bkt = bkt_ref[d]

                def bucket_body(b, tile, bkt=bkt, h=h):
                    return jnp.where(bkt == b, (rb_ref[b, h] - rb_ref[far, h]) * LOG2E, tile)

                bias_ref[h, d] = lax.fori_loop(0, N_BUCKETS, bucket_body, jnp.zeros((tq, tq), F32))

    wi = wi_ref[...]
    qih = [_mask_head(qi_ref[:, (h // 2) * LANES:(h // 2 + 1) * LANES], h % 2)
           for h in range(IDX_HEADS)]

    def score_block(j, nb=1):
        start = pl.multiple_of(j * tq, tq)
        kib = ki_ref[pl.ds(start, tq * nb), :]
        rels = [_dot_nt(kib, qih[h]) for h in range(IDX_HEADS)]
        sc = jnp.zeros((tq * nb, tq), F32)
        for h in range(IDX_HEADS):
            sc = sc + jnp.maximum(rels[h], 0.0) * wi[h:h + 1, :]
        return sc

    def store_scores(j, sc):
        sc_ref[j] = sc
        hi_ref[j] = sc.astype(BF16)

    def score_group(j, nb):
        sc = score_block(j, nb)
        for i in range(nb):
            store_scores(j + i, sc[i * tq:(i + 1) * tq])

    def p1_body(i, c):
        score_group(P1_GROUP * i, P1_GROUP)
        return c

    n_groups1 = qi // P1_GROUP
    lax.fori_loop(0, n_groups1, p1_body, 0)
    done1 = n_groups1 * P1_GROUP
    size = P1_GROUP // 2
    while size >= 1:
        take = (qi // size) % 2 == 1

        @pl.when(take)
        def _(done1=done1, size=size):
            score_group(done1, size)

        done1 = done1 + jnp.where(take, size, 0)
        size //= 2

    key_i = lax.broadcasted_iota(jnp.int32, (tq, tq), 0)
    qry_i = lax.broadcasted_iota(jnp.int32, (tq, tq), 1)
    admissible = key_i < (qry_i // CHUNK + 1) * CHUNK
    store_scores(qi, jnp.where(admissible, score_block(qi), -jnp.inf))

    def count(pred):
        def body(i, acc):
            for j in (2 * i, 2 * i + 1):
                ones = jnp.where(pred(sc_ref[j], j * tq), 1.0, 0.0)
                acc = acc + _tree_sum([ones[r:r + SUBLANES, :] for r in range(0, tq, SUBLANES)])
            return acc
        acc = lax.fori_loop(0, (nblk + 1) // 2, body, jnp.zeros((SUBLANES, tq), F32))
        return jnp.sum(acc, axis=0, keepdims=True)

    def count16(ref, pred):
        rows = 2 * SUBLANES
        def body(i, acc):
            for blk in (ref[2 * i], ref[2 * i + 1]):
                ones = jnp.where(pred(blk), jnp.int16(1), jnp.int16(0))
                acc = acc + _tree_sum([ones[r:r + rows, :] for r in range(0, tq, rows)])
            return acc
        acc = lax.fori_loop(0, (nblk + 1) // 2, body, jnp.zeros((rows, tq), jnp.int16))
        return jnp.sum(acc.astype(F32), axis=0, keepdims=True)

    def count_hi(cand):
        return count16(hi_ref, lambda blk: blk >= cand)

    @pl.when(nblk % 2 == 1)
    def _():
        sc_ref[nblk] = jnp.full((tq, tq), -jnp.inf, F32)
        hi_ref[nblk] = jnp.full((tq, tq), -jnp.inf, BF16)

    def float_of_key(key):
        return lax.bitcast_convert_type(jnp.where(key < 0, key ^ 0x7FFFFFFF, key), F32)

    def key32_of_key16(key16):
        return key16 * (2 * HALF16) + jnp.where(key16 < 0, 2 * HALF16 - 1, 0)

    scanned = (nblk * tq).astype(F32)
    kf = float(topk)
    assert seq // (2 * SUBLANES) < 2 ** 15

    def hi_body(i, prefix):
        cand_u = prefix | jnp.left_shift(jnp.int32(1), 15 - i)
        key16 = cand_u - HALF16
        cand = float_of_key(key32_of_key16(key16)).astype(BF16)
        cnt = jnp.where(key16 <= NEG_INF_KEY16, scanned, count_hi(cand))
        return jnp.where(cnt >= kf, cand_u, prefix)

    hi_key = lax.fori_loop(0, 16, hi_body, jnp.zeros((1, tq), jnp.int32)) - HALF16
    finite_thr = hi_key > NEG_INF_KEY16
    base = jnp.where(finite_thr, key32_of_key16(hi_key) - HALF16, 0)
    base_val = float_of_key(base)
    base_cnt = count(lambda blk, k0: blk >= base_val)

    def lo_body(i, carry):
        off, cnt_ge, cnt_gt = carry
        off_try = off | jnp.left_shift(jnp.int32(1), 16 - i)
        cand = float_of_key(base + off_try)
        cnt = count(lambda blk, k0: blk >= cand)
        take = cnt >= kf
        return jnp.where(take, off_try, off), jnp.where(take, cnt, cnt_ge), jnp.where(take, cnt_gt, cnt)

    off, cnt_ge, cnt_gt = lax.fori_loop(
        0, 17, lo_body, (jnp.zeros((1, tq), jnp.int32), base_cnt, jnp.zeros((1, tq), F32)))
    thr = float_of_key(base + off)

    tie = jnp.logical_and(cnt_ge > kf, finite_thr)
    has_tie = jnp.max(jnp.where(tie, 1.0, 0.0)) > 0.0

    @pl.when(jnp.logical_not(has_tie))
    def _():
        thr_min = jnp.where(finite_thr, thr, MOST_NEGATIVE)

        def mask_body(j, c):
            sc_ref[j] = jnp.where(sc_ref[j] >= thr_min, 0.0, NEG_MASK)
            return c

        lax.fori_loop(0, nblk, mask_body, 0)

    @pl.when(has_tie)
    def _():
        need = kf - cnt_gt
        nbits = int(seq - 1).bit_length()
        assert seq < NO_INDEX

        def eq_body(j, c):
            idx = jnp.where(sc_ref[j] == thr, key_i + j * tq, NO_INDEX)
            eq_ref[j] = idx.astype(jnp.int16)
            return c

        lax.fori_loop(0, nblk, eq_body, 0)

        @pl.when(nblk % 2 == 1)
        def _():
            eq_ref[nblk] = jnp.full((tq, tq), NO_INDEX, jnp.int16)

        def idx_body(i, lo):
            c_try = lo + jnp.left_shift(jnp.int32(1), nbits - 1 - i)
            c16 = c_try.astype(jnp.int16)
            f = count16(eq_ref, lambda blk: blk < c16)
            return jnp.where(f < need, c_try, lo)

        lo = lax.fori_loop(0, nbits, idx_body, jnp.zeros((1, tq), jnp.int32))
        cut = jnp.where(tie, lo + 1, jnp.where(finite_thr, NO_CUT, 0))
        thr_gt = jnp.where(finite_thr, thr, -jnp.inf)

        def mask_body(j, c):
            blk = sc_ref[j]
            at_thr = jnp.where((key_i + j * tq) < cut, 0.0, NEG_MASK)
            sc_ref[j] = jnp.where(blk > thr_gt, 0.0, jnp.where(blk == thr_gt, at_thr, NEG_MASK))
            return c

        lax.fori_loop(0, nblk, mask_body, 0)

    qh = [_mask_head(q_ref[:, (h // 2) * LANES:(h // 2 + 1) * LANES], h % 2)
          for h in range(HEADS_B)]
    m_ref[...] = jnp.full(m_ref.shape, NEG_MASK, F32)
    acc_ref[...] = jnp.zeros(acc_ref.shape, F32)

    def att_block(j, nb, d, stream=False):
        start = pl.multiple_of(j * tq, tq)
        cat = lambda xs, axis: xs[0] if nb == 1 else jnp.concatenate(xs, axis=axis)
        mask = cat([sc_ref[j + i] for i in range(nb)], 0)
        ones = jnp.ones((DENOM_ROWS, tq * nb), BF16)
        vts = [jnp.concatenate(
            [cat([vt_ref[j + i, h * HEAD_DIM:(h + 1) * HEAD_DIM, :] for i in range(nb)], 1), ones], axis=0)
            for h in range(HEADS_B)]
        pair = lambda h: slice((h // HEADS_PER_TILE) * LANES, (h // HEADS_PER_TILE + 1) * LANES)
        qk = lambda h: _dot_nt(k_ref[pl.ds(start, tq * nb), pair(h)], qh[h])
        m_old = m_ref[...]
        hs = range(HEADS_B)

        def biased_logits(group):
            lgs = [qk(h) + mask for h in group]
            if d is not None:
                lgs = [lg + bias_ref[h, d] for lg, h in zip(lgs, group)]
            return lgs

        if stream:
            logits = biased_logits(hs)
            ps = [jnp.exp2(logits[h] - m_old[h:h + 1, :]).astype(BF16) for h in hs]
            bmax = jnp.concatenate([jnp.max(logits[h], axis=0, keepdims=True) for h in hs], axis=0)
            pvs = [_dot(vts[h], ps[h]) for h in hs]
            top_old = top_ref[...]
            accs = [acc_ref[h] + pvs[h] for h in hs]
            top_ref[...] = jnp.maximum(top_old, bmax)
            for h in hs:
                acc_ref[h] = accs[h]
            return
        logits = biased_logits(hs)
        m_new = [jnp.maximum(m_old[h:h + 1, :], jnp.max(logits[h], axis=0, keepdims=True)) for h in hs]
        ps = [jnp.exp2(logits[h] - m_new[h]).astype(BF16) for h in hs]
        pvs = [_dot(vts[h], ps[h]) for h in hs]
        accs = [jnp.exp2(m_old[h:h + 1, :] - m_new[h]) * acc_ref[h] + pvs[h] for h in hs]
        m_ref[...] = jnp.concatenate(m_new, axis=0)
        for h in range(HEADS_B):
            acc_ref[h] = accs[h]

    n_far = qi - 1

    @pl.when(qi > 0)
    def _():
        att_block(qi - 1, 1, 1)
        top_ref[...] = m_ref[...]
        att_block(qi, 1, 0, stream=True)

    @pl.when(qi == 0)
    def _():
        att_block(qi, 1, 0)
        top_ref[...] = m_ref[...]

    def far_body(i, c):
        att_block(FAR_GROUP * i, FAR_GROUP, None, stream=True)
        return c

    n_groups = jnp.maximum(n_far, 0) // FAR_GROUP
    lax.fori_loop(0, n_groups, far_body, 0)
    done = n_groups * FAR_GROUP
    size = FAR_GROUP // 2
    while size >= 1:
        take = jnp.logical_and(n_far > 0, (n_far // size) % 2 == 1)

        @pl.when(take)
        def _(done=done, size=size):
            att_block(done, size, None, stream=True)

        done = done + jnp.where(take, size, 0)
        size //= 2

    @pl.when(jnp.max(top_ref[...] - m_ref[...]) > SOFTMAX_SLACK)
    def _():
        m_ref[...] = jnp.full(m_ref.shape, NEG_MASK, F32)
        acc_ref[...] = jnp.zeros(acc_ref.shape, F32)

        def exact_far(j, c):
            att_block(j, 1, None)
            return c

        lax.fori_loop(0, n_far, exact_far, 0)

        @pl.when(qi > 0)
        def _():
            att_block(qi - 1, 1, 1)

        att_block(qi, 1, 0)

    for hp in range(HEADS_B // HEADS_PER_TILE):
        h0 = hp * HEADS_PER_TILE
        a0, a1 = acc_ref[h0], acc_ref[h0 + 1]
        ot = jnp.concatenate([a0[:HEAD_DIM] / a0[HEAD_DIM:HEAD_DIM + 1, :],
                              a1[:HEAD_DIM] / a1[HEAD_DIM:HEAD_DIM + 1, :]], axis=0)
        gp = g_ref[:, hp * LANES:(hp + 1) * LANES]
        o_ref[:, hp * LANES:(hp + 1) * LANES] = (ot.T * _silu(gp)).astype(o_ref.dtype)


def _t5_bucket(rel):
    half = N_BUCKETS // 2
    max_exact = half // 2
    ret = jnp.where(rel > 0, half, 0)
    n = jnp.abs(rel)
    nf = jnp.maximum(n, 1).astype(jnp.float32)
    large = max_exact + (jnp.log(nf / max_exact) / math.log(MAX_DISTANCE / max_exact)
                         * (half - max_exact)).astype(jnp.int32)
    large = jnp.minimum(large, half - 1)
    return ret + jnp.where(n < max_exact, n, large)


def _dsa_attention(p3, g3, vt, wi_t, rel_bias, tq, topk):
    b, s, _ = p3.shape
    nkb = s // tq
    assert tq >= MAX_DISTANCE
    key = jnp.arange(tq, dtype=jnp.int32)[:, None]
    qry = jnp.arange(tq, dtype=jnp.int32)[None, :]
    buckets = jnp.stack([_t5_bucket(key - qry), _t5_bucket(key - qry - tq)])
    far = _t5_bucket(jnp.full((1,), -tq - 1, jnp.int32))
    buckets = jnp.bitwise_and(buckets, 2 * N_BUCKETS - 1)
    far = jnp.bitwise_and(far, 2 * N_BUCKETS - 1)
    row = lambda blk: (lambda bi, qi: (bi, qi, blk))
    full = lambda blk: (lambda bi, qi: (bi, 0, blk))
    smem = pl.BlockSpec(memory_space=pltpu.SMEM)
    return pl.pallas_call(
        functools.partial(_dsa_kernel, tq=tq, topk=topk, seq=s),
        out_shape=jax.ShapeDtypeStruct((b, s, WIDTH_B), BF16),
        grid=(b, nkb),
        in_specs=[
            smem, smem,
            pl.BlockSpec((None, tq, 512), row(P_QI // 512)),
            pl.BlockSpec((T_WI_ROWS, tq), lambda bi, qi: (0, bi * nkb + qi)),
            pl.BlockSpec((None, s, LANES), full(P_KI // LANES)),
            pl.BlockSpec((None, tq, 512), row(P_QB // 512)),
            pl.BlockSpec((None, s, 512), full(P_KB // 512)),
            pl.BlockSpec((nkb, WIDTH_B, tq), lambda bi, qi: (bi, T_VB // WIDTH_B, 0)),
            pl.BlockSpec((2, tq, tq), lambda bi, qi: (0, 0, 0)),
            pl.BlockSpec((None, tq, 512), row(G_GB // 512)),
        ],
        out_specs=pl.BlockSpec((None, tq, WIDTH_B), lambda bi, qi: (bi, qi, 0)),
        scratch_shapes=[pltpu.VMEM((nkb + nkb % 2, tq, tq), F32),
                        pltpu.VMEM((nkb + nkb % 2, tq, tq), BF16),
                        pltpu.VMEM((nkb + nkb % 2, tq, tq), jnp.int16),
                        pltpu.VMEM((HEADS_B, 2, tq, tq), F32),
                        pltpu.VMEM((HEADS_B, tq), F32),
                        pltpu.VMEM((HEADS_B, tq), F32),
                        pltpu.VMEM((HEADS_B, HEAD_DIM + DENOM_ROWS, tq), F32)],
        compiler_params=pltpu.CompilerParams(
            dimension_semantics=("arbitrary", "arbitrary"),
            vmem_limit_bytes=VMEM_LIMIT_BYTES),
        name="dsa_attention",
    )(rel_bias, far, p3, wi_t, p3, p3, p3, vt, buckets, g3)


def _out_kernel(ma_ref, mb_ref, x_ref, w_ref, lg_ref, lb_ref, o_ref, *, alpha):
    y = _dot(ma_ref[...], w_ref[:WIDTH_A, :]) + _dot(mb_ref[...], w_ref[WIDTH_A:, :])
    h = alpha * x_ref[...] + y
    mu = jnp.mean(h, axis=-1, keepdims=True)
    hc = h - mu
    var = jnp.mean(hc * hc, axis=-1, keepdims=True)
    o_ref[...] = hc * lax.rsqrt(var + LN_EPS) * lg_ref[...] + lb_ref[...]


def _out_proj_ln(mix_a, mix_b, x2, w_out, ln_g, ln_b, alpha, tm):
    m, d = x2.shape
    return pl.pallas_call(
        functools.partial(_out_kernel, alpha=alpha),
        out_shape=jax.ShapeDtypeStruct((m, d), F32),
        grid=(m // tm,),
        in_specs=[pl.BlockSpec((tm, WIDTH_A), lambda i: (i, 0)),
                  pl.BlockSpec((tm, WIDTH_B), lambda i: (i, 0)),
                  pl.BlockSpec((tm, d), lambda i: (i, 0)),
                  pl.BlockSpec((WIDTH_A + WIDTH_B, d), lambda i: (0, 0)),
                  pl.BlockSpec((1, d), lambda i: (0, 0)),
                  pl.BlockSpec((1, d), lambda i: (0, 0))],
        out_specs=pl.BlockSpec((tm, d), lambda i: (i, 0)),
        compiler_params=pltpu.CompilerParams(
            dimension_semantics=("arbitrary",), vmem_limit_bytes=VMEM_LIMIT_BYTES),
        name="out_proj_ln",
    )(mix_a, mix_b, x2, w_out, ln_g, ln_b)


def _pack_w_in(w):
    scale = LOG2E / math.sqrt(HEAD_DIM)
    sizes = (WIDTH_A,) * 4 + (WIDTH_B,) * 4 + (IDX_HEADS * IDX_DIM, IDX_DIM, IDX_HEADS)
    qa, ka, va, ga, qb, kb, vb, gb, qi, ki, wi = jnp.split(w, np.cumsum(sizes)[:-1].tolist(), axis=1)
    w_row = jnp.concatenate([qa * scale, ka, qb * scale, kb, qi, ki, ki, ga, gb], axis=1)
    wi_pad = jnp.pad(wi, ((0, 0), (0, T_WI_ROWS - IDX_HEADS)))
    w_t = jnp.concatenate([va, vb, wi_pad], axis=1).T
    return w_row.astype(BF16), w_t.astype(BF16)


def _layer(x, w_in, w_out, ln_g, ln_b, rel_bias, alpha, topk):
    b, s, d = x.shape
    tq = min(256, s)
    tm = min(512, b * s)
    tm_out = 2 * tm if (b * s) % (2 * tm) == 0 else tm
    assert s % tq == 0 and (b * s) % tm == 0 and (b * s) % tm_out == 0 and tm % tq == 0 and topk <= tq
    x2 = x.reshape(b * s, d)
    w_row, w_t = _pack_w_in(w_in)
    p2, g2, vt, wi_t = _proj(x2, w_row, w_t, tm=tm, tk=tq)
    p3 = p2.reshape(b, s, P_COLS)
    g3 = g2.reshape(b, s, G_COLS)
    mix_a = _sb_attention(p3, g3, vt, tq)
    mix_b = _dsa_attention(p3, g3, vt, wi_t, rel_bias, tq, topk)
    out = _out_proj_ln(mix_a.reshape(b * s, WIDTH_A), mix_b.reshape(b * s, WIDTH_B), x2,
                       w_out.astype(BF16), ln_g.reshape(1, d), ln_b.reshape(1, d), alpha, tm=tm_out)
    return out.reshape(b, s, d)


def kernel(x, w_in, w_out, ln_g, ln_b, rel_bias):
    depth = w_in.shape[0]
    alpha = (2.0 * depth) ** 0.25
    topk = min(TOPK_MAX, x.shape[1] // 4)
    h = x
    for layer in range(depth):
        h = _layer(h, w_in[layer], w_out[layer], ln_g[layer], ln_b[layer], rel_bias, alpha, topk)
    return h
```
